```python
import math
import jax, jax.numpy as jnp
from jax import lax
import numpy as np

D_MODEL = 1024
BATCH = 16
SEQ = 4096
DEPTH = 1
DEC_BATCH = 8
DEC_SEQ = 2048
PAST_LEN = 128

HEAD_DIM = 64
ATT_HEADS = 8
DILATION_GROUPS = ((128, 1), (512, 4), (2048, 16))
N_DIL = len(DILATION_GROUPS)
ATT_WIDTH = ATT_HEADS * HEAD_DIM
BAND_BLOCK = max(w // (2 * d) for w, d in DILATION_GROUPS)
ROT_DIM = HEAD_DIM // 4
ROPE_THETA = 500000.0
SSM_GROUP_CH = 16
SSM_GROUPS = 16
SSM_WIDTH = SSM_GROUPS * SSM_GROUP_CH
SSM_STATE = 64
SSM_MIN_DECAY = 1e-4
MIX_WIDTH = ATT_WIDTH + SSM_WIDTH
ATT_COLS = N_DIL * 3 * ATT_WIDTH
IN_COLS = ATT_COLS + SSM_WIDTH
N_EXPERTS = 32
TOP_K = 4
D_FF = D_MODEL
SWIGLU_LIMIT = 7.0
SWIGLU_ALPHA = 1.702
N_MOD = 6
EPS = 1e-6
MASK_VALUE = -1e30

kernel_name = "hybrid_dilated_attn_s5_moe_encoder"


def rmsnorm(x, g):
    xf = x.astype(jnp.float32)
    y = xf * lax.rsqrt(jnp.mean(xf * xf, axis=-1, keepdims=True) + EPS)
    return (y * g.astype(jnp.float32)).astype(x.dtype)


def rope_tables(s):
    pos = jnp.arange(s, dtype=jnp.float32)
    inv = 1.0 / (ROPE_THETA ** (jnp.arange(0, ROT_DIM, 2, dtype=jnp.float32) / ROT_DIM))
    ang = pos[:, None] * inv[None, :]
    return jnp.cos(ang)[:, None, :], jnp.sin(ang)[:, None, :]


def apply_partial_rope(x, cos, sin):
    half = ROT_DIM // 2
    cos = cos.astype(x.dtype)
    sin = sin.astype(x.dtype)
    x1 = x[..., :half]
    x2 = x[..., half:ROT_DIM]
    return jnp.concatenate([x1 * cos - x2 * sin, x2 * cos + x1 * sin, x[..., ROT_DIM:]], axis=-1)


def band_attention(q, k, v, radius):
    n, l, h, dh = q.shape
    blk = BAND_BLOCK
    nb = -(-l // blk)
    lp = nb * blk
    qb = jnp.pad(q, ((0, 0), (0, lp - l), (0, 0), (0, 0))).reshape(n, nb, blk, h, dh)

    def windows(t):
        tp = jnp.pad(t, ((0, 0), (blk, lp - l + blk), (0, 0), (0, 0))).reshape(n, nb + 2, blk, h, dh)
        return jnp.concatenate([tp[:, :-2], tp[:, 1:-1], tp[:, 2:]], axis=2)

    kw = windows(k)
    vw = windows(v)
    s = jnp.einsum('nbqhd,nbkhd->nbhqk', qb, kw).astype(jnp.float32) * (dh ** -0.5)
    qpos = jnp.arange(nb)[:, None, None] * blk + jnp.arange(blk)[None, :, None]
    kpos = jnp.arange(nb)[:, None, None] * blk - blk + jnp.arange(3 * blk)[None, None, :]
    valid = (jnp.abs(kpos - qpos) <= radius) & (kpos >= 0) & (kpos < l)
    s = jnp.where(valid[None, :, None], s, MASK_VALUE)
    lse = jax.nn.logsumexp(s, axis=-1)
    p = jnp.exp(s - lse[..., None])
    o = jnp.einsum('nbhqk,nbkhd->nbqhd', p.astype(v.dtype), vw).reshape(n, lp, h, dh)[:, :l]
    lse = lse.transpose(0, 1, 3, 2).reshape(n, lp, h)[:, :l]
    return o, lse


def dilated_attention(q, k, v, window, dilation):
    b, s, h, dh = q.shape
    m = s // dilation

    def stride(t):
        return t.reshape(b, m, dilation, h, dh).transpose(0, 2, 1, 3, 4).reshape(b * dilation, m, h, dh)

    o, lse = band_attention(stride(q), stride(k), stride(v), window // (2 * dilation))
    o = o.reshape(b, dilation, m, h, dh).transpose(0, 2, 1, 3, 4).reshape(b, s, h, dh)
    lse = lse.reshape(b, dilation, m, h).transpose(0, 2, 1, 3).reshape(b, s, h)
    return o, lse


def ssm_discretise(a_re, a_im, log_step, b_re, b_im):
    a_re = jnp.minimum(a_re, -SSM_MIN_DECAY)
    dt = jnp.exp(log_step)[:, None]
    mag = jnp.exp(a_re * dt)
    ab_re = mag * jnp.cos(a_im * dt)
    ab_im = mag * jnp.sin(a_im * dt)
    den = a_re * a_re + a_im * a_im
    xr = ab_re - 1.0
    z_re = (xr * a_re + ab_im * a_im) / den
    z_im = (ab_im * a_re - xr * a_im) / den
    bb_re = z_re[..., None] * b_re - z_im[..., None] * b_im
    bb_im = z_re[..., None] * b_im + z_im[..., None] * b_re
    return ab_re, ab_im, bb_re, bb_im


def _linear_combine(left, right):
    a1r, a1i, b1r, b1i = left
    a2r, a2i, b2r, b2i = right
    return (a2r * a1r - a2i * a1i,
            a2r * a1i + a2i * a1r,
            a2r * b1r - a2i * b1i + b2r,
            a2r * b1i + a2i * b1r + b2i)


def ssm_direction(u, a_re, a_im, log_step, b_re, b_im, c_re, c_im, reverse):
    f32 = jnp.float32
    ab_re, ab_im, bb_re, bb_im = ssm_discretise(a_re.astype(f32), a_im.astype(f32), log_step.astype(f32),
                                                b_re.astype(f32), b_im.astype(f32))
    bu_re = jnp.einsum('bsgc,gpc->bsgp', u, bb_re)
    bu_im = jnp.einsum('bsgc,gpc->bsgp', u, bb_im)
    ar = jnp.broadcast_to(ab_re, bu_re.shape)
    ai = jnp.broadcast_to(ab_im, bu_re.shape)
    _, _, h_re, h_im = lax.associative_scan(_linear_combine, (ar, ai, bu_re, bu_im), reverse=reverse, axis=1)
    return (jnp.einsum('bsgp,gcp->bsgc', h_re, c_re.astype(f32))
            - jnp.einsum('bsgp,gcp->bsgc', h_im, c_im.astype(f32)))


def clamped_swiglu(hh):
    x_glu = jnp.minimum(hh[..., ::2], SWIGLU_LIMIT)
    x_lin = jnp.clip(hh[..., 1::2], -SWIGLU_LIMIT, SWIGLU_LIMIT)
    return x_glu * jax.nn.sigmoid(SWIGLU_ALPHA * x_glu) * (x_lin + 1.0)


def moe_ffn(t, w_router, b_router, w_mlp1, b_mlp1, w_mlp2, b_mlp2):
    logits = (t @ w_router + b_router).astype(jnp.float32)
    top_val, top_idx = lax.top_k(logits, TOP_K)
    top_w = jax.nn.softmax(top_val, axis=-1)
    gates = jnp.sum(jax.nn.one_hot(top_idx, N_EXPERTS, dtype=jnp.float32) * top_w[..., None], axis=1)
    gates = gates.astype(t.dtype)
    out = jnp.zeros_like(t)
    for e in range(N_EXPERTS):
        hh = t @ w_mlp1[e] + b_mlp1[e]
        out = out + gates[:, e:e + 1] * (clamped_swiglu(hh) @ w_mlp2[e] + b_mlp2[e])
    return out


def encoder_layer(x, c, p):
    b, s, d = x.shape
    mod = jax.nn.silu(c) @ p['w_ada'] + p['b_ada']
    sh1, sc1, g1, sh2, sc2, g2 = jnp.split(mod[:, None, :], N_MOD, axis=-1)

    h = rmsnorm(x, p['norm1_g']) * (1.0 + sc1) + sh1
    proj = h @ p['w_in']
    qkv = proj[..., :ATT_COLS].reshape(b, s, N_DIL, 3, ATT_HEADS, HEAD_DIM)
    u = proj[..., ATT_COLS:]
    cos, sin = rope_tables(s)
    outs = []
    lses = []
    for g, (window, dil) in enumerate(DILATION_GROUPS):
        q = apply_partial_rope(rmsnorm(qkv[:, :, g, 0], p['q_norm_g'][g]), cos, sin)
        k = apply_partial_rope(rmsnorm(qkv[:, :, g, 1], p['k_norm_g'][g]), cos, sin)
        o, lse = dilated_attention(q, k, qkv[:, :, g, 2], window, dil)
        outs.append(o)
        lses.append(lse)
    wts = jax.nn.softmax(jnp.stack(lses), axis=0)
    att = jnp.einsum('gbsh,gbshd->bshd', wts.astype(x.dtype), jnp.stack(outs)).reshape(b, s, ATT_WIDTH)

    uf = u.astype(jnp.float32).reshape(b, s, SSM_GROUPS, SSM_GROUP_CH)
    y = (ssm_direction(uf, p['ssm_a_re'][0], p['ssm_a_im'][0], p['ssm_log_step'][0], p['ssm_b_re'][0],
                       p['ssm_b_im'][0], p['ssm_c_re'][0], p['ssm_c_im'][0], False)
         + ssm_direction(uf, p['ssm_a_re'][1], p['ssm_a_im'][1], p['ssm_log_step'][1], p['ssm_b_re'][1],
                         p['ssm_b_im'][1], p['ssm_c_re'][1], p['ssm_c_im'][1], True)
         + p['ssm_d'].astype(jnp.float32).reshape(SSM_GROUPS, SSM_GROUP_CH) * uf)
    y = jax.nn.gelu(y.reshape(b, s, SSM_WIDTH)).astype(x.dtype)
    ssm_out = y * jax.nn.sigmoid(y @ p['w_glu'] + p['b_glu'])

    mix = jnp.concatenate([att, ssm_out], axis=-1) @ p['w_out']
    x = x + g1 * mix

    h2 = rmsnorm(x, p['norm2_g']) * (1.0 + sc2) + sh2
    ff = moe_ffn(h2.reshape(b * s, d), p['w_router'], p['b_router'], p['w_mlp1'], p['b_mlp1'],
                 p['w_mlp2'], p['b_mlp2']).reshape(b, s, d)
    return x + g2 * ff


def setup_inputs(seed: int = 0) -> dict:
    key = jax.random.key(seed)
    ks = jax.random.split(key, 32)
    f32 = jnp.float32
    L, D, G, P, C = DEPTH, D_MODEL, SSM_GROUPS, SSM_STATE, SSM_GROUP_CH
    nrm = lambda k, shape, scale: jax.random.normal(k, shape, f32) * scale
    a_im_init = jnp.pi * jnp.arange(P, dtype=f32)
    return {
        'x_prompt': nrm(ks[0], (BATCH, SEQ, D), 1.0),
        'x_sample': nrm(ks[1], (DEC_BATCH, DEC_SEQ, D), 1.0),
        'c_prompt': nrm(ks[2], (BATCH, D), 1.0),
        'c_sample': nrm(ks[3], (DEC_BATCH, D), 1.0),
        'w_ada': nrm(ks[4], (L, D, N_MOD * D), 0.5 * D ** -0.5),
        'b_ada': nrm(ks[5], (L, N_MOD * D), 0.01),
        'norm1_g': 1.0 + nrm(ks[6], (L, D), 0.02),
        'norm2_g': 1.0 + nrm(ks[7], (L, D), 0.02),
        'w_in': nrm(ks[8], (L, D, IN_COLS), D ** -0.5),
        'q_norm_g': 1.0 + nrm(ks[9], (L, N_DIL, HEAD_DIM), 0.02),
        'k_norm_g': 1.0 + nrm(ks[10], (L, N_DIL, HEAD_DIM), 0.02),
        'ssm_a_re': -0.5 + nrm(ks[11], (L, 2, G, P), 0.01),
        'ssm_a_im': a_im_init + nrm(ks[12], (L, 2, G, P), 0.01),
        'ssm_log_step': jax.random.uniform(ks[13], (L, 2, G), f32, math.log(0.001), math.log(0.1)),
        'ssm_b_re': nrm(ks[14], (L, 2, G, P, C), C ** -0.5),
        'ssm_b_im': nrm(ks[15], (L, 2, G, P, C), C ** -0.5),
        'ssm_c_re': nrm(ks[16], (L, 2, G, C, P), (2 * P) ** -0.5),
        'ssm_c_im': nrm(ks[17], (L, 2, G, C, P), (2 * P) ** -0.5),
        'ssm_d': nrm(ks[18], (L, SSM_WIDTH), 1.0),
        'w_glu': nrm(ks[19], (L, SSM_WIDTH, SSM_WIDTH), SSM_WIDTH ** -0.5),
        'b_glu': nrm(ks[20], (L, SSM_WIDTH), 0.01),
        'w_out': nrm(ks[21], (L, MIX_WIDTH, D), MIX_WIDTH ** -0.5),
        'w_router': nrm(ks[22], (L, D, N_EXPERTS), D ** -0.5),
        'b_router': nrm(ks[23], (L, N_EXPERTS), 0.01),
        'w_mlp1': nrm(ks[24], (L, N_EXPERTS, D, 2 * D_FF), D ** -0.5),
        'b_mlp1': nrm(ks[25], (L, N_EXPERTS, 2 * D_FF), 0.01),
        'w_mlp2': nrm(ks[26], (L, N_EXPERTS, D_FF, D), D_FF ** -0.5),
        'b_mlp2': nrm(ks[27], (L, N_EXPERTS, D), 0.01),
    }


def reference(x_prompt, x_sample, c_prompt, c_sample, w_ada, b_ada, norm1_g, norm2_g, w_in, q_norm_g, k_norm_g,
              ssm_a_re, ssm_a_im, ssm_log_step, ssm_b_re, ssm_b_im, ssm_c_re, ssm_c_im, ssm_d, w_glu, b_glu,
              w_out, w_router, b_router, w_mlp1, b_mlp1, w_mlp2, b_mlp2):
    y_prompt = x_prompt
    y_sample = x_sample
    for l in range(DEPTH):
        p = dict(w_ada=w_ada[l], b_ada=b_ada[l], norm1_g=norm1_g[l], norm2_g=norm2_g[l], w_in=w_in[l],
                 q_norm_g=q_norm_g[l], k_norm_g=k_norm_g[l], ssm_a_re=ssm_a_re[l], ssm_a_im=ssm_a_im[l],
                 ssm_log_step=ssm_log_step[l], ssm_b_re=ssm_b_re[l], ssm_b_im=ssm_b_im[l],
                 ssm_c_re=ssm_c_re[l], ssm_c_im=ssm_c_im[l], ssm_d=ssm_d[l], w_glu=w_glu[l], b_glu=b_glu[l],
                 w_out=w_out[l], w_router=w_router[l], b_router=b_router[l], w_mlp1=w_mlp1[l],
                 b_mlp1=b_mlp1[l], w_mlp2=w_mlp2[l], b_mlp2=b_mlp2[l])
        y_prompt = encoder_layer(y_prompt, c_prompt, p)
        y_sample = encoder_layer(y_sample, c_sample, p)
    return (y_prompt, y_sample)
```

```python
import functools
import math

import jax
import jax.numpy as jnp
from jax import lax
from jax.experimental import pallas as pl
from jax.experimental.pallas import tpu as pltpu

F32 = jnp.float32
BF16 = jnp.bfloat16
HIGHEST = lax.Precision.HIGHEST

D_MODEL = 1024
HEAD_DIM = 64
ATT_HEADS = 8
DILATION_GROUPS = ((128, 1), (512, 4), (2048, 16))
N_DIL = len(DILATION_GROUPS)
ATT_WIDTH = ATT_HEADS * HEAD_DIM
QKV_WIDTH = 3 * ATT_WIDTH
BAND_RADIUS = 64
ROT_DIM = HEAD_DIM // 4
ROPE_THETA = 500000.0
SSM_GROUP_CH = 16
SSM_GROUPS = 16
SSM_WIDTH = SSM_GROUPS * SSM_GROUP_CH
SSM_STATE = 64
SSM_LANES = SSM_GROUPS * SSM_STATE
SSM_MIN_DECAY = 1e-4
ATT_COLS = N_DIL * QKV_WIDTH
N_EXPERTS = 32
TOP_K = 4
D_FF = D_MODEL
SWIGLU_LIMIT = 7.0
SWIGLU_ALPHA = 1.702
N_MOD = 6
EPS = 1e-6
MASK_VALUE = -1e30

LANES = 128
VMEM_LIMIT = 56 * 1024 * 1024


def _cparams(*sem):
    return pltpu.CompilerParams(dimension_semantics=sem, vmem_limit_bytes=VMEM_LIMIT)


def _full(shape):
    n = len(shape)
    return pl.BlockSpec(shape, lambda *_: (0,) * n)


def _full_once(shape):
    n = len(shape)
    return pl.BlockSpec(shape, lambda *_: (0,) * n, pipeline_mode=pl.Buffered(1))


def _mod_kernel(c_ref, w_ref, b_ref, o_ref):
    c = c_ref[...]
    a = c * jax.nn.sigmoid(c)
    o_ref[...] = jnp.dot(a, w_ref[...], precision=HIGHEST, preferred_element_type=F32) + b_ref[...]


def _modulation(c_all, w_ada, b_ada):
    nb = c_all.shape[0]
    ncol = w_ada.shape[1]
    bw = D_MODEL
    return pl.pallas_call(
        _mod_kernel,
        grid=(ncol // bw,),
        in_specs=[_full((nb, D_MODEL)),
                  pl.BlockSpec((D_MODEL, bw), lambda j: (0, j)),
                  pl.BlockSpec((1, bw), lambda j: (0, j))],
        out_specs=pl.BlockSpec((nb, bw), lambda j: (0, j)),
        out_shape=jax.ShapeDtypeStruct((nb, ncol), F32),
        compiler_params=_cparams("arbitrary"),
        name="modulation",
    )(c_all, w_ada, b_ada.reshape(1, ncol))


def _head_rms(t, gmat):
    sq = (t * t).astype(BF16)
    half = 2 * LANES
    ss = jnp.concatenate(
        [jnp.dot(sq[:, :half], gmat, preferred_element_type=F32),
         jnp.dot(sq[:, half:], gmat, preferred_element_type=F32)], axis=1)
    return t * lax.rsqrt(ss * (1.0 / HEAD_DIM) + EPS)


def _rope(t, cs, s1, s2):
    return (t * cs + pltpu.roll(t, ATT_WIDTH - ROT_DIM // 2, 1) * s1
            + pltpu.roll(t, ROT_DIM // 2, 1) * s2)


def _proj_kernel(x_ref, sc_ref, sh_ref, g_ref, w_ref, qg_ref, kg_ref, cs_ref, s1_ref, s2_ref, gmat_ref,
                 o0_ref, o1_ref, o2_ref, u_ref):
    x = x_ref[0]
    ms = jnp.mean(x * x, axis=-1, keepdims=True)
    h = (x * lax.rsqrt(ms + EPS) * g_ref[...]) * (1.0 + sc_ref[0]) + sh_ref[0]
    hb = h.astype(BF16)
    rep = ATT_WIDTH // LANES
    cs = jnp.concatenate([cs_ref[...]] * rep, axis=1)
    s1 = jnp.concatenate([s1_ref[...]] * rep, axis=1)
    s2 = jnp.concatenate([s2_ref[...]] * rep, axis=1)
    gmat = gmat_ref[...]
    for g, o_ref in enumerate((o0_ref, o1_ref, o2_ref)):
        p = jnp.dot(hb, w_ref[:, g * QKV_WIDTH:(g + 1) * QKV_WIDTH], preferred_element_type=F32)
        q = _rope(_head_rms(p[:, :ATT_WIDTH], gmat) * qg_ref[g:g + 1, :], cs, s1, s2)
        k = _rope(_head_rms(p[:, ATT_WIDTH:2 * ATT_WIDTH], gmat) * kg_ref[g:g + 1, :], cs, s1, s2)
        o_ref[0, :, 0:ATT_WIDTH] = q.astype(BF16)
        o_ref[0, :, ATT_WIDTH:2 * ATT_WIDTH] = k.astype(BF16)
        o_ref[0, :, 2 * ATT_WIDTH:] = p[:, 2 * ATT_WIDTH:].astype(BF16)
    u_ref[...] = jnp.dot(hb, w_ref[:, ATT_COLS:], preferred_element_type=F32)


def _in_projection(x, sc1, sh1, prep):
    b, s, d = x.shape
    ts = min(512, s)
    qkv_shape = jax.ShapeDtypeStruct((b, s, QKV_WIDTH), BF16)
    qkv_spec = pl.BlockSpec((1, ts, QKV_WIDTH), lambda bi, si: (bi, si, 0))
    vec_spec = pl.BlockSpec((1, 1, d), lambda bi, si: (bi, 0, 0))
    tab_spec = pl.BlockSpec((ts, LANES), lambda bi, si: (si, 0))
    return pl.pallas_call(
        _proj_kernel,
        grid=(b, s // ts),
        in_specs=[pl.BlockSpec((1, ts, d), lambda bi, si: (bi, si, 0)), vec_spec, vec_spec,
                  _full((1, d)), _full_once(prep["w_in"].shape),
                  _full((N_DIL, ATT_WIDTH)), _full((N_DIL, ATT_WIDTH)),
                  tab_spec, tab_spec, tab_spec, _full((2 * LANES, 2 * LANES))],
        out_specs=[qkv_spec, qkv_spec, qkv_spec,
                   pl.BlockSpec((ts, SSM_WIDTH), lambda bi, si: (si, bi))],
        out_shape=[qkv_shape, qkv_shape, qkv_shape,
                   jax.ShapeDtypeStruct((s, b * SSM_WIDTH), F32)],
        compiler_params=_cparams("parallel", "parallel"),
        name="in_projection",
    )(x, sc1, sh1, prep["norm1_g"], prep["w_in"], prep["q_gain"], prep["k_gain"],
      prep["rope_c"][:s], prep["rope_s1"][:s], prep["rope_s2"][:s], prep["head_ind"])


def _attn_kernel(q_ref, k_ref, v_ref, o_ref, lse_ref, *, m, tq, qb, win):
    qi = pl.program_id(2)
    lane = lax.broadcasted_iota(jnp.int32, (qb, LANES), 1)
    lower = lane < HEAD_DIM
    row_i = lax.broadcasted_iota(jnp.int32, (qb, win), 0)
    col_i = lax.broadcasted_iota(jnp.int32, (qb, win), 1)

    def block(j, carry):
        r0 = pl.multiple_of(j * qb, qb)
        q0 = qi * tq + r0
        ws = jnp.clip(q0 - BAND_RADIUS, 0, m - win)
        ws = pl.multiple_of(ws, math.gcd(BAND_RADIUS, win))
        valid = jnp.abs((ws + col_i) - (q0 + row_i)) <= BAND_RADIUS
        qblk = q_ref[0, pl.ds(r0, qb), :]
        kblk = k_ref[0, pl.ds(ws, win), :]
        vblk = v_ref[0, pl.ds(ws, win), :]
        lse_blk = jnp.zeros((qb, LANES), F32)
        for pair in range(ATT_WIDTH // LANES):
            sl = slice(pair * LANES, (pair + 1) * LANES)
            q2, k2, v2 = qblk[:, sl], kblk[:, sl], vblk[:, sl]
            outs = []
            for a in range(2):
                head_lanes = lower if a == 0 else jnp.logical_not(lower)
                qa = jnp.where(head_lanes, q2, jnp.zeros_like(q2))
                sc = lax.dot_general(qa, k2, (((1,), (1,)), ((), ())), preferred_element_type=F32)
                sc = jnp.where(valid, sc, MASK_VALUE)
                mx = jnp.max(sc, axis=1, keepdims=True)
                p = jnp.exp(sc - mx)
                den = jnp.sum(p, axis=1, keepdims=True)
                pv = jnp.dot(p.astype(BF16), v2, preferred_element_type=F32)
                outs.append(pv * (1.0 / den))
                lse_blk = jnp.where(lane == 2 * pair + a, mx + jnp.log(den), lse_blk)
            o_ref[0, pl.ds(r0, qb), sl] = jnp.where(lower, outs[0], outs[1]).astype(BF16)
        lse_ref[0, pl.ds(r0, qb), :] = lse_blk
        return carry

    lax.fori_loop(0, tq // qb, block, 0)


def _band_attention(qkv, dil):
    b, s, _ = qkv.shape
    m = s // dil
    tq = min(512, m)
    qb = min(128, tq)
    win = min(qb + 2 * BAND_RADIUS, m)
    view = qkv.reshape(b, m, dil * QKV_WIDTH)
    n3 = QKV_WIDTH // ATT_WIDTH
    kern = functools.partial(_attn_kernel, m=m, tq=tq, qb=qb, win=win)
    o, lse = pl.pallas_call(
        kern,
        grid=(b, dil, m // tq),
        in_specs=[pl.BlockSpec((1, tq, ATT_WIDTH), lambda bi, r, qi: (bi, qi, n3 * r)),
                  pl.BlockSpec((1, m, ATT_WIDTH), lambda bi, r, qi: (bi, 0, n3 * r + 1)),
                  pl.BlockSpec((1, m, ATT_WIDTH), lambda bi, r, qi: (bi, 0, n3 * r + 2))],
        out_specs=[pl.BlockSpec((1, tq, ATT_WIDTH), lambda bi, r, qi: (bi, qi, r)),
                   pl.BlockSpec((1, tq, LANES), lambda bi, r, qi: (bi, qi, r))],
        out_shape=[jax.ShapeDtypeStruct((b, m, dil * ATT_WIDTH), BF16),
                   jax.ShapeDtypeStruct((b, m, dil * LANES), F32)],
        compiler_params=_cparams("parallel", "parallel", "arbitrary"),
        name=f"band_attention_d{dil}",
    )(view, view, view)
    return o.reshape(b, s, ATT_WIDTH), lse.reshape(b, s, LANES)


SCAN_LANES = 256


def _ssm_kernel(uf_ref, ub_ref, bf_ref, bb_ref, cf_ref, cb_ref, af_ref, ab_ref, yf_ref, yb_ref,
                hf_s, hb_s, carf_s, carb_s, *, nb, tc):
    rows = nb * tc

    @pl.when(pl.program_id(0) == 0)
    def _():
        carf_s[...] = jnp.zeros_like(carf_s)
        carb_s[...] = jnp.zeros_like(carb_s)

    hf_s[...] = jnp.dot(uf_ref[...].reshape(rows, SSM_WIDTH).astype(BF16), bf_ref[...],
                        preferred_element_type=F32)
    hb_s[...] = jnp.dot(ub_ref[...].reshape(rows, SSM_WIDTH).astype(BF16), bb_ref[...],
                        preferred_element_type=F32)

    for c in range(SSM_LANES // SCAN_LANES):
        re = slice(c * SCAN_LANES, (c + 1) * SCAN_LANES)
        im = slice(SSM_LANES + c * SCAN_LANES, SSM_LANES + (c + 1) * SCAN_LANES)
        far = jnp.broadcast_to(af_ref[0:1, re], (nb, SCAN_LANES))
        fai = jnp.broadcast_to(af_ref[1:2, re], (nb, SCAN_LANES))
        bar = jnp.broadcast_to(ab_ref[0:1, re], (nb, SCAN_LANES))
        bai = jnp.broadcast_to(ab_ref[1:2, re], (nb, SCAN_LANES))

        def step(t, carry):
            fr, fi, br, bi = carry
            rf = pl.ds(pl.multiple_of(t * nb, nb), nb)
            rb = pl.ds(pl.multiple_of((tc - 1 - t) * nb, nb), nb)
            nfr = far * fr - fai * fi + hf_s[rf, re]
            nfi = far * fi + fai * fr + hf_s[rf, im]
            nbr = bar * br - bai * bi + hb_s[rb, re]
            nbi = bar * bi + bai * br + hb_s[rb, im]
            hf_s[rf, re] = nfr
            hf_s[rf, im] = nfi
            hb_s[rb, re] = nbr
            hb_s[rb, im] = nbi
            return nfr, nfi, nbr, nbi

        fr, fi, br, bi = lax.fori_loop(
            0, tc, step, (carf_s[:, re], carf_s[:, im], carb_s[:, re], carb_s[:, im]), unroll=2)
        carf_s[:, re] = fr
        carf_s[:, im] = fi
        carb_s[:, re] = br
        carb_s[:, im] = bi

    yf_ref[...] = jnp.dot(hf_s[...].astype(BF16), cf_ref[...],
                          preferred_element_type=F32).reshape(tc, nb, SSM_WIDTH)
    yb_ref[...] = jnp.dot(hb_s[...].astype(BF16), cb_ref[...],
                          preferred_element_type=F32).reshape(tc, nb, SSM_WIDTH)


def _ssm_scan(u_t, prep, nb):
    s = u_t.shape[0]
    u3 = u_t.reshape(s, nb, SSM_WIDTH)
    tc = min(1024 // nb, s)
    nchunk = s // tc
    kern = functools.partial(_ssm_kernel, nb=nb, tc=tc)
    blk = (tc, nb, SSM_WIDTH)
    yf, yb = pl.pallas_call(
        kern,
        grid=(nchunk,),
        in_specs=[pl.BlockSpec(blk, lambda i: (i, 0, 0)),
                  pl.BlockSpec(blk, lambda i: (nchunk - 1 - i, 0, 0)),
                  _full((SSM_WIDTH, 2 * SSM_LANES)), _full((SSM_WIDTH, 2 * SSM_LANES)),
                  _full((2 * SSM_LANES, SSM_WIDTH)), _full((2 * SSM_LANES, SSM_WIDTH)),
                  _full((2, SSM_LANES)), _full((2, SSM_LANES))],
        out_specs=[pl.BlockSpec(blk, lambda i: (i, 0, 0)),
                   pl.BlockSpec(blk, lambda i: (nchunk - 1 - i, 0, 0))],
        out_shape=[jax.ShapeDtypeStruct((s, nb, SSM_WIDTH), F32)] * 2,
        scratch_shapes=[pltpu.VMEM((nb * tc, 2 * SSM_LANES), F32),
                        pltpu.VMEM((nb * tc, 2 * SSM_LANES), F32),
                        pltpu.VMEM((nb, 2 * SSM_LANES), F32),
                        pltpu.VMEM((nb, 2 * SSM_LANES), F32)],
        compiler_params=_cparams("arbitrary"),
        name="s5_scan",
    )(u3, u3, prep["ssm_b"][0], prep["ssm_b"][1], prep["ssm_c"][0], prep["ssm_c"][1],
      prep["ssm_a"][0], prep["ssm_a"][1])
    return yf.reshape(s, nb * SSM_WIDTH), yb.reshape(s, nb * SSM_WIDTH)


def _split_dot(w, mat):
    hi = w.astype(BF16)
    lo = (w - hi.astype(F32)).astype(BF16)
    return (jnp.dot(hi, mat, preferred_element_type=F32)
            + jnp.dot(lo, mat, preferred_element_type=F32))


def _post_kernel(x_ref, o0_ref, o1_ref, o2_ref, l0_ref, l1_ref, l2_ref, yf_ref, yb_ref, u_ref,
                 g1_ref, dvec_ref, wglu_ref, bglu_ref, woa_ref, wos_ref, emat_ref, x1_ref):
    lses = [l0_ref[0], l1_ref[0], l2_ref[0]]
    top = jnp.maximum(jnp.maximum(lses[0], lses[1]), lses[2])
    es = [jnp.exp(l - top) for l in lses]
    inv = 1.0 / (es[0] + es[1] + es[2])
    emat = emat_ref[...]
    att = None
    for e, o_ref in zip(es, (o0_ref, o1_ref, o2_ref)):
        term = _split_dot(e * inv, emat) * o_ref[0].astype(F32)
        att = term if att is None else att + term
    y = yf_ref[...] + yb_ref[...] + dvec_ref[...] * u_ref[...]
    y = 0.5 * y * (1.0 + jnp.tanh(math.sqrt(2.0 / math.pi) * (y + 0.044715 * (y * y * y))))
    z = jnp.dot(y.astype(BF16), wglu_ref[...], preferred_element_type=F32) + bglu_ref[...]
    ssm = y * jax.nn.sigmoid(z)
    mix = (jnp.dot(att.astype(BF16), woa_ref[...], preferred_element_type=F32)
           + jnp.dot(ssm.astype(BF16), wos_ref[...], preferred_element_type=F32))
    x1_ref[0] = x_ref[0] + g1_ref[0] * mix


def _post_mix(x, outs, lses, yf, yb, u_t, g1, prep):
    b, s, d = x.shape
    ts = min(512, s)
    tok = lambda w: pl.BlockSpec((1, ts, w), lambda bi, si: (bi, si, 0))
    tmaj = pl.BlockSpec((ts, SSM_WIDTH), lambda bi, si: (si, bi))
    return pl.pallas_call(
        _post_kernel,
        grid=(b, s // ts),
        in_specs=[tok(d), tok(ATT_WIDTH), tok(ATT_WIDTH), tok(ATT_WIDTH),
                  tok(LANES), tok(LANES), tok(LANES), tmaj, tmaj, tmaj,
                  pl.BlockSpec((1, 1, d), lambda bi, si: (bi, 0, 0)),
                  _full((1, SSM_WIDTH)), _full((SSM_WIDTH, SSM_WIDTH)), _full((1, SSM_WIDTH)),
                  _full((ATT_WIDTH, d)), _full((SSM_WIDTH, d)), _full((LANES, ATT_WIDTH))],
        out_specs=tok(d),
        out_shape=jax.ShapeDtypeStruct((b, s, d), F32),
        compiler_params=_cparams("parallel", "parallel"),
        name="post_mix",
    )(x, *outs, *lses, yf, yb, u_t, g1, prep["ssm_d"], prep["w_glu"], prep["b_glu"],
      prep["w_out_att"], prep["w_out_ssm"], prep["head_expand"])


RANK_BLOCK = 256


def _router_kernel(x1_ref, sc_ref, sh_ref, g_ref, wr_ref, br_ref, ltri_ref,
                   h2_ref, rcol_ref, rt_ref, gt_ref, cnt_ref, *, tt):
    x = x1_ref[...]
    ms = jnp.mean(x * x, axis=-1, keepdims=True)
    h = (x * lax.rsqrt(ms + EPS) * g_ref[...]) * (1.0 + sc_ref[0]) + sh_ref[0]
    h2_ref[...] = h.astype(BF16)
    lane = lax.broadcasted_iota(jnp.int32, (tt, LANES), 1)
    logits = jnp.dot(h, wr_ref[...], precision=HIGHEST, preferred_element_type=F32) + br_ref[...]
    work = jnp.where(lane < N_EXPERTS, logits, -jnp.inf)
    hots, vals = [], []
    for _ in range(TOP_K):
        mx = jnp.max(work, axis=1, keepdims=True)
        idx = jnp.min(jnp.where(work == mx, lane, LANES), axis=1, keepdims=True)
        hot = lane == idx
        hots.append(hot)
        vals.append(mx)
        work = jnp.where(hot, -jnp.inf, work)
    ws = [jnp.exp(v - vals[0]) for v in vals]
    inv = 1.0 / (ws[0] + ws[1] + ws[2] + ws[3])
    gates = jnp.zeros((tt, LANES), F32)
    sel = jnp.zeros((tt, LANES), F32)
    for hot, w in zip(hots, ws):
        gates = jnp.where(hot, w * inv, gates)
        sel = jnp.where(hot, 1.0, sel)
    ltri = ltri_ref[...]
    carry = jnp.zeros((1, LANES), F32)
    parts = []
    for blk in range(tt // RANK_BLOCK):
        sb = sel[blk * RANK_BLOCK:(blk + 1) * RANK_BLOCK]
        parts.append(jnp.dot(ltri, sb.astype(BF16), preferred_element_type=F32) + carry)
        carry = carry + jnp.sum(sb, axis=0, keepdims=True)
    rank = jnp.where(sel > 0.0, jnp.concatenate(parts, axis=0), -1.0)
    rcol_ref[...] = rank
    rt_ref[0] = rank.T[:N_EXPERTS]
    gt_ref[0] = gates.T[:N_EXPERTS]
    cnt_ref[0] = carry.astype(jnp.int32)


def _router(x1f, sc2, sh2, prep, tt, tiles_per_b):
    n, d = x1f.shape
    nt = n // tt
    vec = pl.BlockSpec((1, 1, d), lambda t: (t // tiles_per_b, 0, 0))
    kern = functools.partial(_router_kernel, tt=tt)
    return pl.pallas_call(
        kern,
        grid=(nt,),
        in_specs=[pl.BlockSpec((tt, d), lambda t: (t, 0)), vec, vec, _full((1, d)),
                  _full((d, LANES)), _full((1, LANES)), _full((RANK_BLOCK, RANK_BLOCK))],
        out_specs=[pl.BlockSpec((tt, d), lambda t: (t, 0)),
                   pl.BlockSpec((tt, LANES), lambda t: (t, 0)),
                   pl.BlockSpec((1, N_EXPERTS, tt), lambda t: (t, 0, 0)),
                   pl.BlockSpec((1, N_EXPERTS, tt), lambda t: (t, 0, 0)),
                   pl.BlockSpec((1, 1, LANES), lambda t: (t, 0, 0))],
        out_shape=[jax.ShapeDtypeStruct((n, d), BF16),
                   jax.ShapeDtypeStruct((n, LANES), F32),
                   jax.ShapeDtypeStruct((nt, N_EXPERTS, tt), F32),
                   jax.ShapeDtypeStruct((nt, N_EXPERTS, tt), F32),
                   jax.ShapeDtypeStruct((nt, 1, LANES), jnp.int32)],
        compiler_params=_cparams("parallel"),
        name="router",
    )(x1f, sc2, sh2, prep["norm2_g"], prep["w_router"], prep["b_router"], prep["ltri"])


def _moe_kernel(cnt_ref, h2_ref, x1_ref, rcol_ref, rt_ref, gt_ref, w1_ref, b1_ref, w2_ref, b2_ref,
                g2_ref, out_ref, *, tt, mc):
    t = pl.program_id(0)
    e = pl.program_id(1)

    @pl.when(e == 0)
    def _():
        out_ref[...] = x1_ref[...]

    n = cnt_ref[t * N_EXPERTS + e]
    lane = lax.broadcasted_iota(jnp.int32, (tt, LANES), 1)
    rcol = jnp.sum(jnp.where(lane == e, rcol_ref[...], 0.0), axis=1, keepdims=True)
    rrow = rt_ref[0, pl.ds(e, 1), :]
    grow = gt_ref[0, pl.ds(e, 1), :]

    for c in range(-(-tt // mc)):
        base = float(c * mc)

        @pl.when(c * mc < n)
        def _():
            ri = lax.broadcasted_iota(jnp.int32, (mc, tt), 0).astype(F32) + base
            hit = rrow == ri
            xg = jnp.dot(jnp.where(hit, 1.0, 0.0).astype(BF16), h2_ref[...],
                         preferred_element_type=F32).astype(BF16)
            hh = jnp.dot(xg, w1_ref[0], preferred_element_type=F32) + b1_ref[0]
            glu = jnp.minimum(hh[:, :D_FF], SWIGLU_LIMIT)
            lin = jnp.clip(hh[:, D_FF:], -SWIGLU_LIMIT, SWIGLU_LIMIT)
            act = glu * jax.nn.sigmoid(SWIGLU_ALPHA * glu) * (lin + 1.0)
            y = jnp.dot(act.astype(BF16), w2_ref[0], preferred_element_type=F32) + b2_ref[0]
            gate = jnp.sum(jnp.where(hit, grow, 0.0), axis=1, keepdims=True)
            y = (y * gate) * g2_ref[0]
            ci = lax.broadcasted_iota(jnp.int32, (tt, mc), 1).astype(F32) + base
            scat = jnp.where(rcol == ci, 1.0, 0.0).astype(BF16)
            out_ref[...] += jnp.dot(scat, y.astype(BF16), preferred_element_type=F32)


def _moe(counts, h2, x1f, rcol, rt, gt, g2, prep, tt, mc, tiles_per_b):
    n, d = x1f.shape
    nt = n // tt
    kern = functools.partial(_moe_kernel, tt=tt, mc=mc)
    grid_spec = pltpu.PrefetchScalarGridSpec(
        num_scalar_prefetch=1,
        grid=(nt, N_EXPERTS),
        in_specs=[pl.BlockSpec((tt, d), lambda t, e, c: (t, 0)),
                  pl.BlockSpec((tt, d), lambda t, e, c: (t, 0), pipeline_mode=pl.Buffered(1)),
                  pl.BlockSpec((tt, LANES), lambda t, e, c: (t, 0)),
                  pl.BlockSpec((1, N_EXPERTS, tt), lambda t, e, c: (t, 0, 0)),
                  pl.BlockSpec((1, N_EXPERTS, tt), lambda t, e, c: (t, 0, 0)),
                  pl.BlockSpec((1, d, 2 * D_FF), lambda t, e, c: (e, 0, 0)),
                  pl.BlockSpec((1, 1, 2 * D_FF), lambda t, e, c: (e, 0, 0)),
                  pl.BlockSpec((1, D_FF, d), lambda t, e, c: (e, 0, 0)),
                  pl.BlockSpec((1, 1, d), lambda t, e, c: (e, 0, 0)),
                  pl.BlockSpec((1, 1, d), lambda t, e, c: (t // tiles_per_b, 0, 0))],
        out_specs=pl.BlockSpec((tt, d), lambda t, e, c: (t, 0)),
    )
    return pl.pallas_call(
        kern,
        grid_spec=grid_spec,
        out_shape=jax.ShapeDtypeStruct((n, d), F32),
        compiler_params=_cparams("parallel", "arbitrary"),
        name="sparse_moe",
    )(counts, h2, x1f, rcol, rt, gt, prep["w_mlp1"], prep["b_mlp1"], prep["w_mlp2"], prep["b_mlp2"], g2)


def _ssm_discretise(a_re, a_im, log_step, b_re, b_im):
    a_re = jnp.minimum(a_re, -SSM_MIN_DECAY)
    dt = jnp.exp(log_step)[:, None]
    mag = jnp.exp(a_re * dt)
    ab_re = mag * jnp.cos(a_im * dt)
    ab_im = mag * jnp.sin(a_im * dt)
    den = a_re * a_re + a_im * a_im
    xr = ab_re - 1.0
    z_re = (xr * a_re + ab_im * a_im) / den
    z_im = (ab_im * a_re - xr * a_im) / den
    bb_re = z_re[..., None] * b_re - z_im[..., None] * b_im
    bb_im = z_re[..., None] * b_im + z_im[..., None] * b_re
    return ab_re, ab_im, bb_re, bb_im


def _prepare(max_s, norm1_g, norm2_g, w_in, q_norm_g, k_norm_g, ssm_a_re, ssm_a_im, ssm_log_step,
             ssm_b_re, ssm_b_im, ssm_c_re, ssm_c_im, ssm_d, w_glu, b_glu, w_out, w_router, b_router,
             w_mlp1, b_mlp1, w_mlp2, b_mlp2):
    g, p, c = SSM_GROUPS, SSM_STATE, SSM_GROUP_CH
    eye = jnp.eye(g, dtype=F32)
    a_l, b_l, c_l = [], [], []
    for dr in range(2):
        ab_re, ab_im, bb_re, bb_im = _ssm_discretise(ssm_a_re[dr], ssm_a_im[dr], ssm_log_step[dr],
                                                     ssm_b_re[dr], ssm_b_im[dr])
        a_l.append(jnp.stack([ab_re.reshape(g * p), ab_im.reshape(g * p)]))
        bd = lambda blk: jnp.einsum("gpc,gh->gchp", blk, eye).reshape(g * c, g * p)
        b_l.append(jnp.concatenate([bd(bb_re), bd(bb_im)], axis=1).astype(BF16))
        cd = lambda blk: jnp.einsum("gcp,gh->gphc", blk, eye).reshape(g * p, g * c)
        c_l.append(jnp.concatenate([cd(ssm_c_re[dr]), -cd(ssm_c_im[dr])], axis=0).astype(BF16))

    pos = jnp.arange(max_s, dtype=F32)
    inv = 1.0 / (ROPE_THETA ** (jnp.arange(0, ROT_DIM, 2, dtype=F32) / ROT_DIM))
    ang = pos[:, None] * inv[None, :]
    cos, sin = jnp.cos(ang), jnp.sin(ang)
    half = ROT_DIM // 2
    rest = HEAD_DIM - ROT_DIM
    zeros = lambda w: jnp.zeros((max_s, w), F32)
    per_head = lambda t: jnp.tile(t, (1, LANES // HEAD_DIM))
    rope_c = per_head(jnp.concatenate([cos, cos, jnp.ones((max_s, rest), F32)], axis=1))
    rope_s1 = per_head(jnp.concatenate([-sin, zeros(half + rest)], axis=1))
    rope_s2 = per_head(jnp.concatenate([zeros(half), sin, zeros(rest)], axis=1))

    idx = jnp.arange(2 * LANES)
    head_ind = (idx[:, None] // HEAD_DIM == idx[None, :] // HEAD_DIM).astype(BF16)
    head_expand = (jnp.arange(LANES)[:, None] == jnp.arange(ATT_WIDTH)[None, :] // HEAD_DIM).astype(BF16)
    ridx = jnp.arange(RANK_BLOCK)
    ltri = (ridx[None, :] < ridx[:, None]).astype(BF16)

    w_in_l = w_in
    return dict(
        norm1_g=norm1_g.reshape(1, D_MODEL), norm2_g=norm2_g.reshape(1, D_MODEL),
        w_in=w_in_l.astype(BF16),
        q_gain=jnp.tile(q_norm_g, (1, ATT_HEADS)) * (HEAD_DIM ** -0.5),
        k_gain=jnp.tile(k_norm_g, (1, ATT_HEADS)),
        rope_c=rope_c, rope_s1=rope_s1, rope_s2=rope_s2, head_ind=head_ind, head_expand=head_expand,
        ltri=ltri, ssm_a=a_l, ssm_b=b_l, ssm_c=c_l,
        ssm_d=ssm_d.reshape(1, SSM_WIDTH), w_glu=w_glu.astype(BF16), b_glu=b_glu.reshape(1, SSM_WIDTH),
        w_out_att=w_out[:ATT_WIDTH].astype(BF16), w_out_ssm=w_out[ATT_WIDTH:].astype(BF16),
        w_router=jnp.pad(w_router, ((0, 0), (0, LANES - N_EXPERTS))),
        b_router=jnp.pad(b_router, (0, LANES - N_EXPERTS)).reshape(1, LANES),
        w_mlp1=jnp.concatenate([w_mlp1[..., 0::2], w_mlp1[..., 1::2]], axis=-1).astype(BF16),
        b_mlp1=jnp.concatenate([b_mlp1[..., 0::2], b_mlp1[..., 1::2]], axis=-1).reshape(N_EXPERTS, 1, 2 * D_FF),
        w_mlp2=w_mlp2.astype(BF16), b_mlp2=b_mlp2.reshape(N_EXPERTS, 1, D_MODEL),
    )


def _moe_tiles(s):
    tt = min(1024, s)
    mc = 160 if tt == 1024 else max(16, tt // 4)
    return tt, mc


def _encoder_layer(x, mod, prep):
    b, s, d = x.shape
    sh1, sc1, g1, sh2, sc2, g2 = [m.reshape(b, 1, d) for m in jnp.split(mod, N_MOD, axis=-1)]
    qkv0, qkv1, qkv2, u_t = _in_projection(x, sc1, sh1, prep)
    outs, lses = [], []
    for qkv, (_, dil) in zip((qkv0, qkv1, qkv2), DILATION_GROUPS):
        o, lse = _band_attention(qkv, dil)
        outs.append(o)
        lses.append(lse)
    yf, yb = _ssm_scan(u_t, prep, b)
    x1 = _post_mix(x, outs, lses, yf, yb, u_t, g1, prep)
    tt, mc = _moe_tiles(s)
    x1f = x1.reshape(b * s, d)
    h2, rcol, rt, gt, cnt = _router(x1f, sc2, sh2, prep, tt, s // tt)
    counts = cnt[:, 0, :N_EXPERTS].reshape(-1)
    out = _moe(counts, h2, x1f, rcol, rt, gt, g2, prep, tt, mc, s // tt)
    return out.reshape(b, s, d)


def kernel(x_prompt, x_sample, c_prompt, c_sample, w_ada, b_ada, norm1_g, norm2_g, w_in, q_norm_g, k_norm_g, ssm_a_re, ssm_a_im, ssm_log_step, ssm_b_re, ssm_b_im, ssm_c_re, ssm_c_im, ssm_d, w_glu, b_glu, w_out, w_router, b_router, w_mlp1, b_mlp1, w_mlp2, b_mlp2):
    depth = w_ada.shape[0]
    y_prompt, y_sample = x_prompt, x_sample
    nbp = x_prompt.shape[0]
    max_s = max(x_prompt.shape[1], x_sample.shape[1])
    for l in range(depth):
        prep = _prepare(max_s, norm1_g[l], norm2_g[l], w_in[l], q_norm_g[l], k_norm_g[l], ssm_a_re[l],
                        ssm_a_im[l], ssm_log_step[l], ssm_b_re[l], ssm_b_im[l], ssm_c_re[l], ssm_c_im[l],
                        ssm_d[l], w_glu[l], b_glu[l], w_out[l], w_router[l], b_router[l], w_mlp1[l],
                        b_mlp1[l], w_mlp2[l], b_mlp2[l])
        mod = _modulation(jnp.concatenate([c_prompt, c_sample], axis=0), w_ada[l], b_ada[l])
        y_prompt = _encoder_layer(y_prompt, mod[:nbp], prep)
        y_sample = _encoder_layer(y_sample, mod[nbp:], prep)
    return (y_prompt, y_sample)
```

```python
import functools
import math

import jax
import jax.numpy as jnp
from jax import lax
from jax.experimental import pallas as pl
from jax.experimental.pallas import tpu as pltpu

F32 = jnp.float32
BF16 = jnp.bfloat16
HIGHEST = lax.Precision.HIGHEST

D_MODEL = 1024
HEAD_DIM = 64
ATT_HEADS = 8
DILATION_GROUPS = ((128, 1), (512, 4), (2048, 16))
N_DIL = len(DILATION_GROUPS)
ATT_WIDTH = ATT_HEADS * HEAD_DIM
QKV_WIDTH = 3 * ATT_WIDTH
BAND_RADIUS = 64
ROT_DIM = HEAD_DIM // 4
ROPE_THETA = 500000.0
SSM_GROUP_CH = 16
SSM_GROUPS = 16
SSM_WIDTH = SSM_GROUPS * SSM_GROUP_CH
SSM_STATE = 64
SSM_LANES = SSM_GROUPS * SSM_STATE
SSM_MIN_DECAY = 1e-4
ATT_COLS = N_DIL * QKV_WIDTH
N_EXPERTS = 32
TOP_K = 4
D_FF = D_MODEL
SWIGLU_LIMIT = 7.0
SWIGLU_ALPHA = 1.702
N_MOD = 6
EPS = 1e-6
MASK_VALUE = -1e30

LANES = 128
VMEM_LIMIT = 56 * 1024 * 1024


def _cparams(*sem):
    return pltpu.CompilerParams(dimension_semantics=sem, vmem_limit_bytes=VMEM_LIMIT)


def _full(shape):
    n = len(shape)
    return pl.BlockSpec(shape, lambda *_: (0,) * n)


def _full_once(shape):
    n = len(shape)
    return pl.BlockSpec(shape, lambda *_: (0,) * n, pipeline_mode=pl.Buffered(1))


def _mod_kernel(c_ref, w_ref, b_ref, o_ref):
    c = c_ref[...]
    a = c * jax.nn.sigmoid(c)
    o_ref[...] = jnp.dot(a, w_ref[...], precision=HIGHEST, preferred_element_type=F32) + b_ref[...]


def _modulation(c_all, w_ada, b_ada):
    nb = c_all.shape[0]
    ncol = w_ada.shape[1]
    bw = D_MODEL
    return pl.pallas_call(
        _mod_kernel,
        grid=(ncol // bw,),
        in_specs=[_full((nb, D_MODEL)),
                  pl.BlockSpec((D_MODEL, bw), lambda j: (0, j)),
                  pl.BlockSpec((1, bw), lambda j: (0, j))],
        out_specs=pl.BlockSpec((nb, bw), lambda j: (0, j)),
        out_shape=jax.ShapeDtypeStruct((nb, ncol), F32),
        compiler_params=_cparams("arbitrary"),
        name="modulation",
    )(c_all, w_ada, b_ada.reshape(1, ncol))


def _head_rms(t, gmat):
    sq = (t * t).astype(BF16)
    half = 2 * LANES
    ss = jnp.concatenate(
        [jnp.dot(sq[:, :half], gmat, preferred_element_type=F32),
         jnp.dot(sq[:, half:], gmat, preferred_element_type=F32)], axis=1)
    return t * lax.rsqrt(ss * (1.0 / HEAD_DIM) + EPS)


def _rope(t, cs, s1, s2):
    return (t * cs + pltpu.roll(t, ATT_WIDTH - ROT_DIM // 2, 1) * s1
            + pltpu.roll(t, ROT_DIM // 2, 1) * s2)


def _store_dilated(o_ref, tmp_ref, col, val, dil):
    if dil == 1:
        o_ref[0, :, col:col + ATT_WIDTH] = val.astype(BF16)
        return
    rows = val.shape[0] // dil
    nchunk = ATT_WIDTH // LANES
    for c in range(nchunk):
        tmp_ref[c] = val[:, c * LANES:(c + 1) * LANES]
    for r in range(dil):
        c0 = r * QKV_WIDTH + col
        picked = [tmp_ref[c, pl.ds(r, rows, stride=dil), :] for c in range(nchunk)]
        o_ref[0, :, c0:c0 + ATT_WIDTH] = jnp.concatenate(picked, axis=1).astype(BF16)


def _proj_kernel(x_ref, sc_ref, sh_ref, g_ref, w_ref, qg_ref, kg_ref, cs_ref, s1_ref, s2_ref, gmat_ref,
                 o0_ref, o1_ref, o2_ref, u_ref, tq_ref, tk_ref, tv_ref):
    x = x_ref[0]
    ms = jnp.mean(x * x, axis=-1, keepdims=True)
    h = (x * lax.rsqrt(ms + EPS) * g_ref[...]) * (1.0 + sc_ref[0]) + sh_ref[0]
    hb = h.astype(BF16)
    rep = ATT_WIDTH // LANES
    cs = jnp.concatenate([cs_ref[...]] * rep, axis=1)
    s1 = jnp.concatenate([s1_ref[...]] * rep, axis=1)
    s2 = jnp.concatenate([s2_ref[...]] * rep, axis=1)
    gmat = gmat_ref[...]
    for g, o_ref in enumerate((o0_ref, o1_ref, o2_ref)):
        p = jnp.dot(hb, w_ref[:, g * QKV_WIDTH:(g + 1) * QKV_WIDTH], preferred_element_type=F32)
        q = _rope(_head_rms(p[:, :ATT_WIDTH], gmat) * qg_ref[g:g + 1, :], cs, s1, s2)
        k = _rope(_head_rms(p[:, ATT_WIDTH:2 * ATT_WIDTH], gmat) * kg_ref[g:g + 1, :], cs, s1, s2)
        dil = DILATION_GROUPS[g][1]
        _store_dilated(o_ref, tq_ref, 0, q, dil)
        _store_dilated(o_ref, tk_ref, ATT_WIDTH, k, dil)
        _store_dilated(o_ref, tv_ref, 2 * ATT_WIDTH, p[:, 2 * ATT_WIDTH:], dil)
    u_ref[...] = jnp.dot(hb, w_ref[:, ATT_COLS:], preferred_element_type=F32)


def _in_projection(x, sc1, sh1, prep):
    b, s, d = x.shape
    ts = min(512, s)
    qkv_shape = [jax.ShapeDtypeStruct((b, s // dl, dl * QKV_WIDTH), BF16) for _, dl in DILATION_GROUPS]
    qkv_spec = [pl.BlockSpec((1, ts // dl, dl * QKV_WIDTH), lambda bi, si: (bi, si, 0))
                for _, dl in DILATION_GROUPS]
    vec_spec = pl.BlockSpec((1, 1, d), lambda bi, si: (bi, 0, 0))
    tab_spec = pl.BlockSpec((ts, LANES), lambda bi, si: (si, 0))
    return pl.pallas_call(
        _proj_kernel,
        grid=(b, s // ts),
        in_specs=[pl.BlockSpec((1, ts, d), lambda bi, si: (bi, si, 0)), vec_spec, vec_spec,
                  _full((1, d)), _full_once(prep["w_in"].shape),
                  _full((N_DIL, ATT_WIDTH)), _full((N_DIL, ATT_WIDTH)),
                  tab_spec, tab_spec, tab_spec, _full((2 * LANES, 2 * LANES))],
        out_specs=qkv_spec + [pl.BlockSpec((ts, SSM_WIDTH), lambda bi, si: (si, bi))],
        out_shape=qkv_shape + [jax.ShapeDtypeStruct((s, b * SSM_WIDTH), F32)],
        scratch_shapes=[pltpu.VMEM((ATT_WIDTH // LANES, ts, LANES), F32)] * 3,
        compiler_params=_cparams("parallel", "parallel"),
        name="in_projection",
    )(x, sc1, sh1, prep["norm1_g"], prep["w_in"], prep["q_gain"], prep["k_gain"],
      prep["rope_c"][:s], prep["rope_s1"][:s], prep["rope_s2"][:s], prep["head_ind"])


ATTN_UNROLL = 4


def _attn_kernel(qkv_ref, o_ref, lse_ref, *, m, qb, win, rp):
    lane = lax.broadcasted_iota(jnp.int32, (qb, LANES), 1)
    lower = lane < HEAD_DIM
    row_i = lax.broadcasted_iota(jnp.int32, (qb, win), 0)
    col_i = lax.broadcasted_iota(jnp.int32, (qb, win), 1)

    for rr in range(rp):
        base = rr * QKV_WIDTH

        def block(j, carry, base=base, rr=rr):
            q0 = pl.multiple_of(j * qb, qb)
            ws = jnp.clip(q0 - BAND_RADIUS, 0, m - win)
            ws = pl.multiple_of(ws, math.gcd(BAND_RADIUS, win))
            valid = jnp.abs((ws + col_i) - (q0 + row_i)) <= BAND_RADIUS
            lse_blk = jnp.zeros((qb, LANES), F32)
            for pair in range(ATT_WIDTH // LANES):
                off = base + pair * LANES
                q2 = qkv_ref[0, pl.ds(q0, qb), off:off + LANES]
                k2 = qkv_ref[0, pl.ds(ws, win), ATT_WIDTH + off:ATT_WIDTH + off + LANES]
                v2 = qkv_ref[0, pl.ds(ws, win), 2 * ATT_WIDTH + off:2 * ATT_WIDTH + off + LANES]
                outs = []
                for a in range(2):
                    head_lanes = lower if a == 0 else jnp.logical_not(lower)
                    qa = jnp.where(head_lanes, q2, jnp.zeros_like(q2))
                    sc = lax.dot_general(qa, k2, (((1,), (1,)), ((), ())), preferred_element_type=F32)
                    sc = jnp.where(valid, sc, MASK_VALUE)
                    mx = jnp.max(sc, axis=1, keepdims=True)
                    p = jnp.exp(sc - mx)
                    den = jnp.sum(p, axis=1, keepdims=True)
                    pv = jnp.dot(p.astype(BF16), v2, preferred_element_type=F32)
                    outs.append(pv * (1.0 / den))
                    lse_blk = jnp.where(lane == 2 * pair + a, mx + jnp.log(den), lse_blk)
                oc = rr * ATT_WIDTH + pair * LANES
                o_ref[0, pl.ds(q0, qb), oc:oc + LANES] = jnp.where(lower, outs[0], outs[1]).astype(BF16)
            lse_ref[0, pl.ds(q0, qb), rr * LANES:(rr + 1) * LANES] = lse_blk
            return carry

        nblk = m // qb
        lax.fori_loop(0, nblk, block, 0, unroll=min(ATTN_UNROLL, nblk))


def _band_attention(view, dil):
    b, m, _ = view.shape
    qb = min(128, m)
    win = min(qb + 2 * BAND_RADIUS, m)
    rp = max(1, min(dil, ATTN_UNROLL * qb // m))
    kern = functools.partial(_attn_kernel, m=m, qb=qb, win=win, rp=rp)
    return pl.pallas_call(
        kern,
        grid=(b, dil // rp),
        in_specs=[pl.BlockSpec((1, m, rp * QKV_WIDTH), lambda bi, r: (bi, 0, r))],
        out_specs=[pl.BlockSpec((1, m, rp * ATT_WIDTH), lambda bi, r: (bi, 0, r)),
                   pl.BlockSpec((1, m, rp * LANES), lambda bi, r: (bi, 0, r))],
        out_shape=[jax.ShapeDtypeStruct((b, m, dil * ATT_WIDTH), BF16),
                   jax.ShapeDtypeStruct((b, m, dil * LANES), F32)],
        compiler_params=_cparams("parallel", "parallel"),
        name=f"band_attention_d{dil}",
    )(view)


SCAN_LANES = 256


def _ssm_kernel(uf_ref, ub_ref, bf_ref, bb_ref, cf_ref, cb_ref, af_ref, ab_ref, yf_ref, yb_ref,
                hf_s, hb_s, carf_s, carb_s, *, nb, tc):
    rows = nb * tc

    @pl.when(pl.program_id(0) == 0)
    def _():
        carf_s[...] = jnp.zeros_like(carf_s)
        carb_s[...] = jnp.zeros_like(carb_s)

    hf_s[...] = jnp.dot(uf_ref[...].reshape(rows, SSM_WIDTH).astype(BF16), bf_ref[...],
                        preferred_element_type=F32)
    hb_s[...] = jnp.dot(ub_ref[...].reshape(rows, SSM_WIDTH).astype(BF16), bb_ref[...],
                        preferred_element_type=F32)

    for c in range(SSM_LANES // SCAN_LANES):
        re = slice(c * SCAN_LANES, (c + 1) * SCAN_LANES)
        im = slice(SSM_LANES + c * SCAN_LANES, SSM_LANES + (c + 1) * SCAN_LANES)
        far = jnp.broadcast_to(af_ref[0:1, re], (nb, SCAN_LANES))
        fai = jnp.broadcast_to(af_ref[1:2, re], (nb, SCAN_LANES))
        bar = jnp.broadcast_to(ab_ref[0:1, re], (nb, SCAN_LANES))
        bai = jnp.broadcast_to(ab_ref[1:2, re], (nb, SCAN_LANES))

        def step(t, carry):
            fr, fi, br, bi = carry
            rf = pl.ds(pl.multiple_of(t * nb, nb), nb)
            rb = pl.ds(pl.multiple_of((tc - 1 - t) * nb, nb), nb)
            nfr = far * fr - fai * fi + hf_s[rf, re]
            nfi = far * fi + fai * fr + hf_s[rf, im]
            nbr = bar * br - bai * bi + hb_s[rb, re]
            nbi = bar * bi + bai * br + hb_s[rb, im]
            hf_s[rf, re] = nfr
            hf_s[rf, im] = nfi
            hb_s[rb, re] = nbr
            hb_s[rb, im] = nbi
            return nfr, nfi, nbr, nbi

        fr, fi, br, bi = lax.fori_loop(
            0, tc, step, (carf_s[:, re], carf_s[:, im], carb_s[:, re], carb_s[:, im]), unroll=2)
        carf_s[:, re] = fr
        carf_s[:, im] = fi
        carb_s[:, re] = br
        carb_s[:, im] = bi

    yf_ref[...] = jnp.dot(hf_s[...].astype(BF16), cf_ref[...],
                          preferred_element_type=F32).reshape(tc, nb, SSM_WIDTH)
    yb_ref[...] = jnp.dot(hb_s[...].astype(BF16), cb_ref[...],
                          preferred_element_type=F32).reshape(tc, nb, SSM_WIDTH)


def _ssm_scan(u_t, prep, nb):
    s = u_t.shape[0]
    u3 = u_t.reshape(s, nb, SSM_WIDTH)
    tc = min(1024 // nb, s)
    nchunk = s // tc
    kern = functools.partial(_ssm_kernel, nb=nb, tc=tc)
    blk = (tc, nb, SSM_WIDTH)
    yf, yb = pl.pallas_call(
        kern,
        grid=(nchunk,),
        in_specs=[pl.BlockSpec(blk, lambda i: (i, 0, 0)),
                  pl.BlockSpec(blk, lambda i: (nchunk - 1 - i, 0, 0)),
                  _full((SSM_WIDTH, 2 * SSM_LANES)), _full((SSM_WIDTH, 2 * SSM_LANES)),
                  _full((2 * SSM_LANES, SSM_WIDTH)), _full((2 * SSM_LANES, SSM_WIDTH)),
                  _full((2, SSM_LANES)), _full((2, SSM_LANES))],
        out_specs=[pl.BlockSpec(blk, lambda i: (i, 0, 0)),
                   pl.BlockSpec(blk, lambda i: (nchunk - 1 - i, 0, 0))],
        out_shape=[jax.ShapeDtypeStruct((s, nb, SSM_WIDTH), F32)] * 2,
        scratch_shapes=[pltpu.VMEM((nb * tc, 2 * SSM_LANES), F32),
                        pltpu.VMEM((nb * tc, 2 * SSM_LANES), F32),
                        pltpu.VMEM((nb, 2 * SSM_LANES), F32),
                        pltpu.VMEM((nb, 2 * SSM_LANES), F32)],
        compiler_params=_cparams("arbitrary"),
        name="s5_scan",
    )(u3, u3, prep["ssm_b"][0], prep["ssm_b"][1], prep["ssm_c"][0], prep["ssm_c"][1],
      prep["ssm_a"][0], prep["ssm_a"][1])
    return yf.reshape(s, nb * SSM_WIDTH), yb.reshape(s, nb * SSM_WIDTH)


def _split_dot(w, mat):
    hi = w.astype(BF16)
    lo = (w - hi.astype(F32)).astype(BF16)
    return (jnp.dot(hi, mat, preferred_element_type=F32)
            + jnp.dot(lo, mat, preferred_element_type=F32))


def _natural_order(ref, tmp_ref, dil, width):
    if dil == 1:
        return ref[0].astype(F32)
    rows = ref.shape[1]
    nchunk = width // LANES
    for r in range(dil):
        blk = ref[0, :, r * width:(r + 1) * width].astype(F32)
        for c in range(nchunk):
            tmp_ref[c, pl.ds(r, rows, stride=dil), :] = blk[:, c * LANES:(c + 1) * LANES]
    return jnp.concatenate([tmp_ref[c] for c in range(nchunk)], axis=1)


def _post_kernel(x_ref, o0_ref, o1_ref, o2_ref, l0_ref, l1_ref, l2_ref, yf_ref, yb_ref, u_ref,
                 g1_ref, dvec_ref, wglu_ref, bglu_ref, woa_ref, wos_ref, emat_ref, x1_ref,
                 to1_ref, to2_ref, tl1_ref, tl2_ref):
    dils = [dl for _, dl in DILATION_GROUPS]
    lses = [_natural_order(r, t, dl, LANES)
            for r, t, dl in zip((l0_ref, l1_ref, l2_ref), (None, tl1_ref, tl2_ref), dils)]
    outs = [_natural_order(r, t, dl, ATT_WIDTH)
            for r, t, dl in zip((o0_ref, o1_ref, o2_ref), (None, to1_ref, to2_ref), dils)]
    top = jnp.maximum(jnp.maximum(lses[0], lses[1]), lses[2])
    es = [jnp.exp(l - top) for l in lses]
    inv = 1.0 / (es[0] + es[1] + es[2])
    emat = emat_ref[...]
    att = None
    for e, o in zip(es, outs):
        term = _split_dot(e * inv, emat) * o
        att = term if att is None else att + term
    y = yf_ref[...] + yb_ref[...] + dvec_ref[...] * u_ref[...]
    y = 0.5 * y * (1.0 + jnp.tanh(math.sqrt(2.0 / math.pi) * (y + 0.044715 * (y * y * y))))
    z = jnp.dot(y.astype(BF16), wglu_ref[...], preferred_element_type=F32) + bglu_ref[...]
    ssm = y * jax.nn.sigmoid(z)
    mix = (jnp.dot(att.astype(BF16), woa_ref[...], preferred_element_type=F32)
           + jnp.dot(ssm.astype(BF16), wos_ref[...], preferred_element_type=F32))
    x1_ref[0] = x_ref[0] + g1_ref[0] * mix


def _post_mix(x, outs, lses, yf, yb, u_t, g1, prep):
    b, s, d = x.shape
    ts = min(512, s)
    tok = lambda w, dl=1: pl.BlockSpec((1, ts // dl, dl * w), lambda bi, si: (bi, si, 0))
    tmaj = pl.BlockSpec((ts, SSM_WIDTH), lambda bi, si: (si, bi))
    dils = [dl for _, dl in DILATION_GROUPS]
    return pl.pallas_call(
        _post_kernel,
        grid=(b, s // ts),
        in_specs=[tok(d)] + [tok(ATT_WIDTH, dl) for dl in dils] + [tok(LANES, dl) for dl in dils]
                 + [tmaj, tmaj, tmaj,
                  pl.BlockSpec((1, 1, d), lambda bi, si: (bi, 0, 0)),
                  _full((1, SSM_WIDTH)), _full((SSM_WIDTH, SSM_WIDTH)), _full((1, SSM_WIDTH)),
                  _full((ATT_WIDTH, d)), _full((SSM_WIDTH, d)), _full((LANES, ATT_WIDTH))],
        out_specs=tok(d),
        out_shape=jax.ShapeDtypeStruct((b, s, d), F32),
        scratch_shapes=([pltpu.VMEM((ATT_WIDTH // LANES, ts, LANES), F32)] * 2
                        + [pltpu.VMEM((1, ts, LANES), F32)] * 2),
        compiler_params=_cparams("parallel", "parallel"),
        name="post_mix",
    )(x, *outs, *lses, yf, yb, u_t, g1, prep["ssm_d"], prep["w_glu"], prep["b_glu"],
      prep["w_out_att"], prep["w_out_ssm"], prep["head_expand"])


RANK_BLOCK = 256


def _router_kernel(x1_ref, sc_ref, sh_ref, g_ref, wr_ref, br_ref, ltri_ref,
                   h2_ref, rcol_ref, rt_ref, gt_ref, cnt_ref, *, tt):
    x = x1_ref[...]
    ms = jnp.mean(x * x, axis=-1, keepdims=True)
    h = (x * lax.rsqrt(ms + EPS) * g_ref[...]) * (1.0 + sc_ref[0]) + sh_ref[0]
    h2_ref[...] = h.astype(BF16)
    lane = lax.broadcasted_iota(jnp.int32, (tt, LANES), 1)
    logits = jnp.dot(h, wr_ref[...], precision=HIGHEST, preferred_element_type=F32) + br_ref[...]
    work = jnp.where(lane < N_EXPERTS, logits, -jnp.inf)
    hots, vals = [], []
    for _ in range(TOP_K):
        mx = jnp.max(work, axis=1, keepdims=True)
        idx = jnp.min(jnp.where(work == mx, lane, LANES), axis=1, keepdims=True)
        hot = lane == idx
        hots.append(hot)
        vals.append(mx)
        work = jnp.where(hot, -jnp.inf, work)
    ws = [jnp.exp(v - vals[0]) for v in vals]
    inv = 1.0 / (ws[0] + ws[1] + ws[2] + ws[3])
    gates = jnp.zeros((tt, LANES), F32)
    sel = jnp.zeros((tt, LANES), F32)
    for hot, w in zip(hots, ws):
        gates = jnp.where(hot, w * inv, gates)
        sel = jnp.where(hot, 1.0, sel)
    ltri = ltri_ref[...]
    carry = jnp.zeros((1, LANES), F32)
    parts = []
    for blk in range(tt // RANK_BLOCK):
        sb = sel[blk * RANK_BLOCK:(blk + 1) * RANK_BLOCK]
        parts.append(jnp.dot(ltri, sb.astype(BF16), preferred_element_type=F32) + carry)
        carry = carry + jnp.sum(sb, axis=0, keepdims=True)
    rank = jnp.where(sel > 0.0, jnp.concatenate(parts, axis=0), -1.0)
    rcol_ref[...] = rank
    rt_ref[0] = rank.T[:N_EXPERTS]
    gt_ref[0] = gates.T[:N_EXPERTS]
    cnt_ref[0] = carry.astype(jnp.int32)


def _router(x1f, sc2, sh2, prep, tt, tiles_per_b):
    n, d = x1f.shape
    nt = n // tt
    vec = pl.BlockSpec((1, 1, d), lambda t: (t // tiles_per_b, 0, 0))
    kern = functools.partial(_router_kernel, tt=tt)
    return pl.pallas_call(
        kern,
        grid=(nt,),
        in_specs=[pl.BlockSpec((tt, d), lambda t: (t, 0)), vec, vec, _full((1, d)),
                  _full((d, LANES)), _full((1, LANES)), _full((RANK_BLOCK, RANK_BLOCK))],
        out_specs=[pl.BlockSpec((tt, d), lambda t: (t, 0)),
                   pl.BlockSpec((tt, LANES), lambda t: (t, 0)),
                   pl.BlockSpec((1, N_EXPERTS, tt), lambda t: (t, 0, 0)),
                   pl.BlockSpec((1, N_EXPERTS, tt), lambda t: (t, 0, 0)),
                   pl.BlockSpec((1, 1, LANES), lambda t: (t, 0, 0))],
        out_shape=[jax.ShapeDtypeStruct((n, d), BF16),
                   jax.ShapeDtypeStruct((n, LANES), F32),
                   jax.ShapeDtypeStruct((nt, N_EXPERTS, tt), F32),
                   jax.ShapeDtypeStruct((nt, N_EXPERTS, tt), F32),
                   jax.ShapeDtypeStruct((nt, 1, LANES), jnp.int32)],
        compiler_params=_cparams("parallel"),
        name="router",
    )(x1f, sc2, sh2, prep["norm2_g"], prep["w_router"], prep["b_router"], prep["ltri"])


def _moe_kernel(cnt_ref, h2_ref, x1_ref, rcol_ref, rt_ref, gt_ref, w1_ref, b1_ref, w2_ref, b2_ref,
                g2_ref, out_ref, *, tt, mc):
    t = pl.program_id(0)
    e = pl.program_id(1)

    @pl.when(e == 0)
    def _():
        out_ref[...] = x1_ref[...]

    n = cnt_ref[t * N_EXPERTS + e]
    lane = lax.broadcasted_iota(jnp.int32, (tt, LANES), 1)
    rsel = jnp.where(lane == e, rcol_ref[...], 0.0)
    rhi = rsel.astype(BF16)
    rlo = (rsel - rhi.astype(F32)).astype(BF16)
    mcp = -(-mc // LANES) * LANES
    ones = jnp.ones((LANES, mcp), BF16)
    rcol = (jnp.dot(rhi, ones, preferred_element_type=F32)
            + jnp.dot(rlo, ones, preferred_element_type=F32))[:, :mc]
    rrow = rt_ref[0, pl.ds(e, 1), :]
    grow = gt_ref[0, pl.ds(e, 1), :]

    for c in range(-(-tt // mc)):
        base = float(c * mc)

        @pl.when(c * mc < n)
        def _():
            ri = lax.broadcasted_iota(jnp.int32, (mc, tt), 0).astype(F32) + base
            hit = rrow == ri
            xg = jnp.dot(jnp.where(hit, 1.0, 0.0).astype(BF16), h2_ref[...],
                         preferred_element_type=F32).astype(BF16)
            hh = jnp.dot(xg, w1_ref[0], preferred_element_type=F32) + b1_ref[0]
            glu = jnp.minimum(hh[:, :D_FF], SWIGLU_LIMIT)
            lin = jnp.clip(hh[:, D_FF:], -SWIGLU_LIMIT, SWIGLU_LIMIT)
            act = glu * jax.nn.sigmoid(SWIGLU_ALPHA * glu) * (lin + 1.0)
            y = jnp.dot(act.astype(BF16), w2_ref[0], preferred_element_type=F32) + b2_ref[0]
            gate = jnp.sum(jnp.where(hit, grow, 0.0), axis=1, keepdims=True)
            y = (y * gate) * g2_ref[0]
            ci = lax.broadcasted_iota(jnp.int32, (tt, mc), 1).astype(F32) + base
            scat = jnp.where(rcol == ci, 1.0, 0.0).astype(BF16)
            out_ref[...] += jnp.dot(scat, y.astype(BF16), preferred_element_type=F32)


def _moe(counts, h2, x1f, rcol, rt, gt, g2, prep, tt, mc, tiles_per_b):
    n, d = x1f.shape
    nt = n // tt
    kern = functools.partial(_moe_kernel, tt=tt, mc=mc)
    grid_spec = pltpu.PrefetchScalarGridSpec(
        num_scalar_prefetch=1,
        grid=(nt, N_EXPERTS),
        in_specs=[pl.BlockSpec((tt, d), lambda t, e, c: (t, 0)),
                  pl.BlockSpec((tt, d), lambda t, e, c: (t, 0), pipeline_mode=pl.Buffered(1)),
                  pl.BlockSpec((tt, LANES), lambda t, e, c: (t, 0)),
                  pl.BlockSpec((1, N_EXPERTS, tt), lambda t, e, c: (t, 0, 0)),
                  pl.BlockSpec((1, N_EXPERTS, tt), lambda t, e, c: (t, 0, 0)),
                  pl.BlockSpec((1, d, 2 * D_FF), lambda t, e, c: (e, 0, 0)),
                  pl.BlockSpec((1, 1, 2 * D_FF), lambda t, e, c: (e, 0, 0)),
                  pl.BlockSpec((1, D_FF, d), lambda t, e, c: (e, 0, 0)),
                  pl.BlockSpec((1, 1, d), lambda t, e, c: (e, 0, 0)),
                  pl.BlockSpec((1, 1, d), lambda t, e, c: (t // tiles_per_b, 0, 0))],
        out_specs=pl.BlockSpec((tt, d), lambda t, e, c: (t, 0)),
    )
    return pl.pallas_call(
        kern,
        grid_spec=grid_spec,
        out_shape=jax.ShapeDtypeStruct((n, d), F32),
        compiler_params=_cparams("parallel", "arbitrary"),
        name="sparse_moe",
    )(counts, h2, x1f, rcol, rt, gt, prep["w_mlp1"], prep["b_mlp1"], prep["w_mlp2"], prep["b_mlp2"], g2)


DEINT_BLOCK = 2 * LANES


def _deint_kernel(w_ref, perm_ref, o_ref):
    perm = perm_ref[...]
    half = DEINT_BLOCK // 2
    for blk in range(2 * D_FF // DEINT_BLOCK):
        wb = w_ref[0, :, blk * DEINT_BLOCK:(blk + 1) * DEINT_BLOCK].astype(BF16)
        res = jnp.dot(wb, perm, preferred_element_type=F32).astype(BF16)
        o_ref[0, :, blk * half:(blk + 1) * half] = res[:, :half]
        o_ref[0, :, D_FF + blk * half:D_FF + (blk + 1) * half] = res[:, half:]


def _deinterleave_mlp1(w_mlp1):
    idx = jnp.arange(DEINT_BLOCK)
    src = jnp.where(idx < DEINT_BLOCK // 2, 2 * idx, 2 * (idx - DEINT_BLOCK // 2) + 1)
    perm = (idx[:, None] == src[None, :]).astype(BF16)
    blk = (1, D_MODEL, 2 * D_FF)
    return pl.pallas_call(
        _deint_kernel,
        grid=(N_EXPERTS,),
        in_specs=[pl.BlockSpec(blk, lambda e: (e, 0, 0)), _full((DEINT_BLOCK, DEINT_BLOCK))],
        out_specs=pl.BlockSpec(blk, lambda e: (e, 0, 0)),
        out_shape=jax.ShapeDtypeStruct(w_mlp1.shape, BF16),
        compiler_params=_cparams("parallel"),
        name="mlp1_regroup",
    )(w_mlp1, perm)


def _ssm_discretise(a_re, a_im, log_step, b_re, b_im):
    a_re = jnp.minimum(a_re, -SSM_MIN_DECAY)
    dt = jnp.exp(log_step)[:, None]
    mag = jnp.exp(a_re * dt)
    ab_re = mag * jnp.cos(a_im * dt)
    ab_im = mag * jnp.sin(a_im * dt)
    den = a_re * a_re + a_im * a_im
    xr = ab_re - 1.0
    z_re = (xr * a_re + ab_im * a_im) / den
    z_im = (ab_im * a_re - xr * a_im) / den
    bb_re = z_re[..., None] * b_re - z_im[..., None] * b_im
    bb_im = z_re[..., None] * b_im + z_im[..., None] * b_re
    return ab_re, ab_im, bb_re, bb_im


def _prepare(max_s, norm1_g, norm2_g, w_in, q_norm_g, k_norm_g, ssm_a_re, ssm_a_im, ssm_log_step,
             ssm_b_re, ssm_b_im, ssm_c_re, ssm_c_im, ssm_d, w_glu, b_glu, w_out, w_router, b_router,
             w_mlp1, b_mlp1, w_mlp2, b_mlp2):
    g, p, c = SSM_GROUPS, SSM_STATE, SSM_GROUP_CH
    eye = jnp.eye(g, dtype=F32)
    a_l, b_l, c_l = [], [], []
    for dr in range(2):
        ab_re, ab_im, bb_re, bb_im = _ssm_discretise(ssm_a_re[dr], ssm_a_im[dr], ssm_log_step[dr],
                                                     ssm_b_re[dr], ssm_b_im[dr])
        a_l.append(jnp.stack([ab_re.reshape(g * p), ab_im.reshape(g * p)]))
        bd = lambda blk: jnp.einsum("gpc,gh->gchp", blk, eye).reshape(g * c, g * p)
        b_l.append(jnp.concatenate([bd(bb_re), bd(bb_im)], axis=1).astype(BF16))
        cd = lambda blk: jnp.einsum("gcp,gh->gphc", blk, eye).reshape(g * p, g * c)
        c_l.append(jnp.concatenate([cd(ssm_c_re[dr]), -cd(ssm_c_im[dr])], axis=0).astype(BF16))

    pos = jnp.arange(max_s, dtype=F32)
    inv = 1.0 / (ROPE_THETA ** (jnp.arange(0, ROT_DIM, 2, dtype=F32) / ROT_DIM))
    ang = pos[:, None] * inv[None, :]
    cos, sin = jnp.cos(ang), jnp.sin(ang)
    half = ROT_DIM // 2
    rest = HEAD_DIM - ROT_DIM
    zeros = lambda w: jnp.zeros((max_s, w), F32)
    per_head = lambda t: jnp.tile(t, (1, LANES // HEAD_DIM))
    rope_c = per_head(jnp.concatenate([cos, cos, jnp.ones((max_s, rest), F32)], axis=1))
    rope_s1 = per_head(jnp.concatenate([-sin, zeros(half + rest)], axis=1))
    rope_s2 = per_head(jnp.concatenate([zeros(half), sin, zeros(rest)], axis=1))

    idx = jnp.arange(2 * LANES)
    head_ind = (idx[:, None] // HEAD_DIM == idx[None, :] // HEAD_DIM).astype(BF16)
    head_expand = (jnp.arange(LANES)[:, None] == jnp.arange(ATT_WIDTH)[None, :] // HEAD_DIM).astype(BF16)
    ridx = jnp.arange(RANK_BLOCK)
    ltri = (ridx[None, :] < ridx[:, None]).astype(BF16)

    return dict(
        norm1_g=norm1_g.reshape(1, D_MODEL), norm2_g=norm2_g.reshape(1, D_MODEL),
        w_in=w_in.astype(BF16),
        q_gain=jnp.tile(q_norm_g, (1, ATT_HEADS)) * (HEAD_DIM ** -0.5),
        k_gain=jnp.tile(k_norm_g, (1, ATT_HEADS)),
        rope_c=rope_c, rope_s1=rope_s1, rope_s2=rope_s2, head_ind=head_ind, head_expand=head_expand,
        ltri=ltri, ssm_a=a_l, ssm_b=b_l, ssm_c=c_l,
        ssm_d=ssm_d.reshape(1, SSM_WIDTH), w_glu=w_glu.astype(BF16), b_glu=b_glu.reshape(1, SSM_WIDTH),
        w_out_att=w_out[:ATT_WIDTH].astype(BF16), w_out_ssm=w_out[ATT_WIDTH:].astype(BF16),
        w_router=jnp.pad(w_router, ((0, 0), (0, LANES - N_EXPERTS))),
        b_router=jnp.pad(b_router, (0, LANES - N_EXPERTS)).reshape(1, LANES),
        w_mlp1=_deinterleave_mlp1(w_mlp1),
        b_mlp1=jnp.concatenate([b_mlp1[..., 0::2], b_mlp1[..., 1::2]], axis=-1).reshape(N_EXPERTS, 1, 2 * D_FF),
        w_mlp2=w_mlp2.astype(BF16), b_mlp2=b_mlp2.reshape(N_EXPERTS, 1, D_MODEL),
    )


def _moe_tiles(s):
    tt = min(1024, s)
    mc = 160 if tt == 1024 else max(16, tt // 4)
    return tt, mc


def _encoder_layer(x, mod, prep):
    b, s, d = x.shape
    sh1, sc1, g1, sh2, sc2, g2 = [m.reshape(b, 1, d) for m in jnp.split(mod, N_MOD, axis=-1)]
    qkv0, qkv1, qkv2, u_t = _in_projection(x, sc1, sh1, prep)
    outs, lses = [], []
    for qkv, (_, dil) in zip((qkv0, qkv1, qkv2), DILATION_GROUPS):
        o, lse = _band_attention(qkv, dil)
        outs.append(o)
        lses.append(lse)
    yf, yb = _ssm_scan(u_t, prep, b)
    x1 = _post_mix(x, outs, lses, yf, yb, u_t, g1, prep)
    tt, mc = _moe_tiles(s)
    x1f = x1.reshape(b * s, d)
    h2, rcol, rt, gt, cnt = _router(x1f, sc2, sh2, prep, tt, s // tt)
    counts = cnt[:, 0, :N_EXPERTS].reshape(-1)
    out = _moe(counts, h2, x1f, rcol, rt, gt, g2, prep, tt, mc, s // tt)
    return out.reshape(b, s, d)


def kernel(x_prompt, x_sample, c_prompt, c_sample, w_ada, b_ada, norm1_g, norm2_g, w_in, q_norm_g, k_norm_g, ssm_a_re, ssm_a_im, ssm_log_step, ssm_b_re, ssm_b_im, ssm_c_re, ssm_c_im, ssm_d, w_glu, b_glu, w_out, w_router, b_router, w_mlp1, b_mlp1, w_mlp2, b_mlp2):
    depth = w_ada.shape[0]
    y_prompt, y_sample = x_prompt, x_sample
    nbp = x_prompt.shape[0]
    max_s = max(x_prompt.shape[1], x_sample.shape[1])
    for l in range(depth):
        prep = _prepare(max_s, norm1_g[l], norm2_g[l], w_in[l], q_norm_g[l], k_norm_g[l], ssm_a_re[l],
                        ssm_a_im[l], ssm_log_step[l], ssm_b_re[l], ssm_b_im[l], ssm_c_re[l], ssm_c_im[l],
                        ssm_d[l], w_glu[l], b_glu[l], w_out[l], w_router[l], b_router[l], w_mlp1[l],
                        b_mlp1[l], w_mlp2[l], b_mlp2[l])
        mod = _modulation(jnp.concatenate([c_prompt, c_sample], axis=0), w_ada[l], b_ada[l])
        y_prompt = _encoder_layer(y_prompt, mod[:nbp], prep)
        y_sample = _encoder_layer(y_sample, mod[nbp:], prep)
    return (y_prompt, y_sample)
```

```python
import functools
import math

import jax
import jax.numpy as jnp
from jax import lax
from jax.experimental import pallas as pl
from jax.experimental.pallas import tpu as pltpu

F32 = jnp.float32
BF16 = jnp.bfloat16
HIGHEST = lax.Precision.HIGHEST

D_MODEL = 1024
HEAD_DIM = 64
ATT_HEADS = 8
DILATION_GROUPS = ((128, 1), (512, 4), (2048, 16))
N_DIL = len(DILATION_GROUPS)
ATT_WIDTH = ATT_HEADS * HEAD_DIM
QKV_WIDTH = 3 * ATT_WIDTH
BAND_RADIUS = 64
ROT_DIM = HEAD_DIM // 4
ROPE_THETA = 500000.0
SSM_GROUP_CH = 16
SSM_GROUPS = 16
SSM_WIDTH = SSM_GROUPS * SSM_GROUP_CH
SSM_STATE = 64
SSM_LANES = SSM_GROUPS * SSM_STATE
SSM_MIN_DECAY = 1e-4
ATT_COLS = N_DIL * QKV_WIDTH
N_EXPERTS = 32
TOP_K = 4
D_FF = D_MODEL
SWIGLU_LIMIT = 7.0
SWIGLU_ALPHA = 1.702
N_MOD = 6
EPS = 1e-6
MASK_VALUE = -1e30

LANES = 128
VMEM_LIMIT = 56 * 1024 * 1024


def _cparams(*sem):
    return pltpu.CompilerParams(dimension_semantics=sem, vmem_limit_bytes=VMEM_LIMIT)


def _full(shape):
    n = len(shape)
    return pl.BlockSpec(shape, lambda *_: (0,) * n)


def _full_once(shape):
    n = len(shape)
    return pl.BlockSpec(shape, lambda *_: (0,) * n, pipeline_mode=pl.Buffered(1))


def _mod_kernel(c_ref, w_ref, b_ref, o_ref):
    c = c_ref[...]
    a = c * jax.nn.sigmoid(c)
    o_ref[...] = jnp.dot(a, w_ref[...], precision=HIGHEST, preferred_element_type=F32) + b_ref[...]


def _modulation(c_all, w_ada, b_ada):
    nb = c_all.shape[0]
    ncol = w_ada.shape[1]
    bw = D_MODEL
    return pl.pallas_call(
        _mod_kernel,
        grid=(ncol // bw,),
        in_specs=[_full((nb, D_MODEL)),
                  pl.BlockSpec((D_MODEL, bw), lambda j: (0, j)),
                  pl.BlockSpec((1, bw), lambda j: (0, j))],
        out_specs=pl.BlockSpec((nb, bw), lambda j: (0, j)),
        out_shape=jax.ShapeDtypeStruct((nb, ncol), F32),
        compiler_params=_cparams("arbitrary"),
        name="modulation",
    )(c_all, w_ada, b_ada.reshape(1, ncol))


def _head_rms(t, gmat):
    sq = (t * t).astype(BF16)
    half = 2 * LANES
    ss = jnp.concatenate(
        [jnp.dot(sq[:, :half], gmat, preferred_element_type=F32),
         jnp.dot(sq[:, half:], gmat, preferred_element_type=F32)], axis=1)
    return t * lax.rsqrt(ss * (1.0 / HEAD_DIM) + EPS)


def _rope(t, cs, s1, s2):
    return (t * cs + pltpu.roll(t, ATT_WIDTH - ROT_DIM // 2, 1) * s1
            + pltpu.roll(t, ROT_DIM // 2, 1) * s2)


def _store_dilated(o_ref, tmp_ref, col, val, dil):
    if dil == 1:
        o_ref[0, :, col:col + ATT_WIDTH] = val.astype(BF16)
        return
    rows = val.shape[0] // dil
    nchunk = ATT_WIDTH // LANES
    for c in range(nchunk):
        tmp_ref[c] = val[:, c * LANES:(c + 1) * LANES]
    for r in range(dil):
        c0 = r * QKV_WIDTH + col
        picked = [tmp_ref[c, pl.ds(r, rows, stride=dil), :] for c in range(nchunk)]
        o_ref[0, :, c0:c0 + ATT_WIDTH] = jnp.concatenate(picked, axis=1).astype(BF16)


def _proj_kernel(x_ref, sc_ref, sh_ref, g_ref, w_ref, qg_ref, kg_ref, cs_ref, s1_ref, s2_ref, gmat_ref,
                 o0_ref, o1_ref, o2_ref, u_ref, tq_ref, tk_ref, tv_ref):
    x = x_ref[0]
    ms = jnp.mean(x * x, axis=-1, keepdims=True)
    h = (x * lax.rsqrt(ms + EPS) * g_ref[...]) * (1.0 + sc_ref[0]) + sh_ref[0]
    hb = h.astype(BF16)
    rep = ATT_WIDTH // LANES
    cs = jnp.concatenate([cs_ref[...]] * rep, axis=1)
    s1 = jnp.concatenate([s1_ref[...]] * rep, axis=1)
    s2 = jnp.concatenate([s2_ref[...]] * rep, axis=1)
    gmat = gmat_ref[...]
    for g, o_ref in enumerate((o0_ref, o1_ref, o2_ref)):
        p = jnp.dot(hb, w_ref[:, g * QKV_WIDTH:(g + 1) * QKV_WIDTH], preferred_element_type=F32)
        q = _rope(_head_rms(p[:, :ATT_WIDTH], gmat) * qg_ref[g:g + 1, :], cs, s1, s2)
        k = _rope(_head_rms(p[:, ATT_WIDTH:2 * ATT_WIDTH], gmat) * kg_ref[g:g + 1, :], cs, s1, s2)
        dil = DILATION_GROUPS[g][1]
        _store_dilated(o_ref, tq_ref, 0, q, dil)
        _store_dilated(o_ref, tk_ref, ATT_WIDTH, k, dil)
        _store_dilated(o_ref, tv_ref, 2 * ATT_WIDTH, p[:, 2 * ATT_WIDTH:], dil)
    u_ref[...] = jnp.dot(hb, w_ref[:, ATT_COLS:], preferred_element_type=F32)


def _in_projection(x, sc1, sh1, prep):
    b, s, d = x.shape
    ts = min(512, s)
    qkv_shape = [jax.ShapeDtypeStruct((b, s // dl, dl * QKV_WIDTH), BF16) for _, dl in DILATION_GROUPS]
    qkv_spec = [pl.BlockSpec((1, ts // dl, dl * QKV_WIDTH), lambda bi, si: (bi, si, 0))
                for _, dl in DILATION_GROUPS]
    vec_spec = pl.BlockSpec((1, 1, d), lambda bi, si: (bi, 0, 0))
    tab_spec = pl.BlockSpec((ts, LANES), lambda bi, si: (si, 0))
    return pl.pallas_call(
        _proj_kernel,
        grid=(b, s // ts),
        in_specs=[pl.BlockSpec((1, ts, d), lambda bi, si: (bi, si, 0)), vec_spec, vec_spec,
                  _full((1, d)), _full_once(prep["w_in"].shape),
                  _full((N_DIL, ATT_WIDTH)), _full((N_DIL, ATT_WIDTH)),
                  tab_spec, tab_spec, tab_spec, _full((2 * LANES, 2 * LANES))],
        out_specs=qkv_spec + [pl.BlockSpec((ts, SSM_WIDTH), lambda bi, si: (si, bi))],
        out_shape=qkv_shape + [jax.ShapeDtypeStruct((s, b * SSM_WIDTH), F32)],
        scratch_shapes=[pltpu.VMEM((ATT_WIDTH // LANES, ts, LANES), F32)] * 3,
        compiler_params=_cparams("parallel", "parallel"),
        name="in_projection",
    )(x, sc1, sh1, prep["norm1_g"], prep["w_in"], prep["q_gain"], prep["k_gain"],
      prep["rope_c"][:s], prep["rope_s1"][:s], prep["rope_s2"][:s], prep["head_ind"])


ATTN_UNROLL = 4


def _attn_kernel(qkv_ref, o_ref, lse_ref, *, m, qb, win, rp):
    lane = lax.broadcasted_iota(jnp.int32, (qb, LANES), 1)
    lower = lane < HEAD_DIM
    row_i = lax.broadcasted_iota(jnp.int32, (qb, win), 0)
    col_i = lax.broadcasted_iota(jnp.int32, (qb, win), 1)

    for rr in range(rp):
        base = rr * QKV_WIDTH

        def block(j, carry, base=base, rr=rr):
            q0 = pl.multiple_of(j * qb, qb)
            ws = jnp.clip(q0 - BAND_RADIUS, 0, m - win)
            ws = pl.multiple_of(ws, math.gcd(BAND_RADIUS, win))
            valid = jnp.abs((ws + col_i) - (q0 + row_i)) <= BAND_RADIUS
            lse_blk = jnp.zeros((qb, LANES), F32)
            for pair in range(ATT_WIDTH // LANES):
                off = base + pair * LANES
                q2 = qkv_ref[0, pl.ds(q0, qb), off:off + LANES]
                k2 = qkv_ref[0, pl.ds(ws, win), ATT_WIDTH + off:ATT_WIDTH + off + LANES]
                v2 = qkv_ref[0, pl.ds(ws, win), 2 * ATT_WIDTH + off:2 * ATT_WIDTH + off + LANES]
                outs = []
                for a in range(2):
                    head_lanes = lower if a == 0 else jnp.logical_not(lower)
                    qa = jnp.where(head_lanes, q2, jnp.zeros_like(q2))
                    sc = lax.dot_general(qa, k2, (((1,), (1,)), ((), ())), preferred_element_type=F32)
                    sc = jnp.where(valid, sc, MASK_VALUE)
                    mx = jnp.max(sc, axis=1, keepdims=True)
                    p = jnp.exp(sc - mx)
                    den = jnp.sum(p, axis=1, keepdims=True)
                    pv = jnp.dot(p.astype(BF16), v2, preferred_element_type=F32)
                    outs.append(pv * (1.0 / den))
                    lse_blk = jnp.where(lane == 2 * pair + a, mx + jnp.log(den), lse_blk)
                oc = rr * ATT_WIDTH + pair * LANES
                o_ref[0, pl.ds(q0, qb), oc:oc + LANES] = jnp.where(lower, outs[0], outs[1]).astype(BF16)
            lse_ref[0, pl.ds(q0, qb), rr * LANES:(rr + 1) * LANES] = lse_blk
            return carry

        nblk = m // qb
        lax.fori_loop(0, nblk, block, 0, unroll=min(ATTN_UNROLL, nblk))


def _band_attention(view, dil):
    b, m, _ = view.shape
    qb = min(128, m)
    win = min(qb + 2 * BAND_RADIUS, m)
    rp = max(1, min(dil, ATTN_UNROLL * qb // m))
    kern = functools.partial(_attn_kernel, m=m, qb=qb, win=win, rp=rp)
    return pl.pallas_call(
        kern,
        grid=(b, dil // rp),
        in_specs=[pl.BlockSpec((1, m, rp * QKV_WIDTH), lambda bi, r: (bi, 0, r))],
        out_specs=[pl.BlockSpec((1, m, rp * ATT_WIDTH), lambda bi, r: (bi, 0, r)),
                   pl.BlockSpec((1, m, rp * LANES), lambda bi, r: (bi, 0, r))],
        out_shape=[jax.ShapeDtypeStruct((b, m, dil * ATT_WIDTH), BF16),
                   jax.ShapeDtypeStruct((b, m, dil * LANES), F32)],
        compiler_params=_cparams("parallel", "parallel"),
        name=f"band_attention_d{dil}",
    )(view)


SCAN_LANES = 256


def _ssm_kernel(uf_ref, ub_ref, bf_ref, bb_ref, cf_ref, cb_ref, af_ref, ab_ref, yf_ref, yb_ref,
                hf_s, hb_s, carf_s, carb_s, *, nb, tc):
    rows = nb * tc

    @pl.when(pl.program_id(0) == 0)
    def _():
        carf_s[...] = jnp.zeros_like(carf_s)
        carb_s[...] = jnp.zeros_like(carb_s)

    hf_s[...] = jnp.dot(uf_ref[...].reshape(rows, SSM_WIDTH).astype(BF16), bf_ref[...],
                        preferred_element_type=F32)
    hb_s[...] = jnp.dot(ub_ref[...].reshape(rows, SSM_WIDTH).astype(BF16), bb_ref[...],
                        preferred_element_type=F32)

    for c in range(SSM_LANES // SCAN_LANES):
        re = slice(c * SCAN_LANES, (c + 1) * SCAN_LANES)
        im = slice(SSM_LANES + c * SCAN_LANES, SSM_LANES + (c + 1) * SCAN_LANES)
        far = jnp.broadcast_to(af_ref[0:1, re], (nb, SCAN_LANES))
        fai = jnp.broadcast_to(af_ref[1:2, re], (nb, SCAN_LANES))
        bar = jnp.broadcast_to(ab_ref[0:1, re], (nb, SCAN_LANES))
        bai = jnp.broadcast_to(ab_ref[1:2, re], (nb, SCAN_LANES))

        def step(t, carry):
            fr, fi, br, bi = carry
            rf = pl.ds(pl.multiple_of(t * nb, nb), nb)
            rb = pl.ds(pl.multiple_of((tc - 1 - t) * nb, nb), nb)
            nfr = far * fr - fai * fi + hf_s[rf, re]
            nfi = far * fi + fai * fr + hf_s[rf, im]
            nbr = bar * br - bai * bi + hb_s[rb, re]
            nbi = bar * bi + bai * br + hb_s[rb, im]
            hf_s[rf, re] = nfr
            hf_s[rf, im] = nfi
            hb_s[rb, re] = nbr
            hb_s[rb, im] = nbi
            return nfr, nfi, nbr, nbi

        fr, fi, br, bi = lax.fori_loop(
            0, tc, step, (carf_s[:, re], carf_s[:, im], carb_s[:, re], carb_s[:, im]), unroll=2)
        carf_s[:, re] = fr
        carf_s[:, im] = fi
        carb_s[:, re] = br
        carb_s[:, im] = bi

    yf_ref[...] = jnp.dot(hf_s[...].astype(BF16), cf_ref[...],
                          preferred_element_type=F32).reshape(tc, nb, SSM_WIDTH)
    yb_ref[...] = jnp.dot(hb_s[...].astype(BF16), cb_ref[...],
                          preferred_element_type=F32).reshape(tc, nb, SSM_WIDTH)


def _ssm_scan(u_t, prep, nb):
    s = u_t.shape[0]
    u3 = u_t.reshape(s, nb, SSM_WIDTH)
    tc = min(1024 // nb, s)
    nchunk = s // tc
    kern = functools.partial(_ssm_kernel, nb=nb, tc=tc)
    blk = (tc, nb, SSM_WIDTH)
    yf, yb = pl.pallas_call(
        kern,
        grid=(nchunk,),
        in_specs=[pl.BlockSpec(blk, lambda i: (i, 0, 0)),
                  pl.BlockSpec(blk, lambda i: (nchunk - 1 - i, 0, 0)),
                  _full((SSM_WIDTH, 2 * SSM_LANES)), _full((SSM_WIDTH, 2 * SSM_LANES)),
                  _full((2 * SSM_LANES, SSM_WIDTH)), _full((2 * SSM_LANES, SSM_WIDTH)),
                  _full((2, SSM_LANES)), _full((2, SSM_LANES))],
        out_specs=[pl.BlockSpec(blk, lambda i: (i, 0, 0)),
                   pl.BlockSpec(blk, lambda i: (nchunk - 1 - i, 0, 0))],
        out_shape=[jax.ShapeDtypeStruct((s, nb, SSM_WIDTH), F32)] * 2,
        scratch_shapes=[pltpu.VMEM((nb * tc, 2 * SSM_LANES), F32),
                        pltpu.VMEM((nb * tc, 2 * SSM_LANES), F32),
                        pltpu.VMEM((nb, 2 * SSM_LANES), F32),
                        pltpu.VMEM((nb, 2 * SSM_LANES), F32)],
        compiler_params=_cparams("arbitrary"),
        name="s5_scan",
    )(u3, u3, prep["ssm_b"][0], prep["ssm_b"][1], prep["ssm_c"][0], prep["ssm_c"][1],
      prep["ssm_a"][0], prep["ssm_a"][1])
    return yf.reshape(s, nb * SSM_WIDTH), yb.reshape(s, nb * SSM_WIDTH)


def _split_dot(w, mat):
    hi = w.astype(BF16)
    lo = (w - hi.astype(F32)).astype(BF16)
    return (jnp.dot(hi, mat, preferred_element_type=F32)
            + jnp.dot(lo, mat, preferred_element_type=F32))


def _natural_order(ref, tmp_ref, dil, width):
    if dil == 1:
        return ref[0].astype(F32)
    rows = ref.shape[1]
    nchunk = width // LANES
    for r in range(dil):
        blk = ref[0, :, r * width:(r + 1) * width].astype(F32)
        for c in range(nchunk):
            tmp_ref[c, pl.ds(r, rows, stride=dil), :] = blk[:, c * LANES:(c + 1) * LANES]
    return jnp.concatenate([tmp_ref[c] for c in range(nchunk)], axis=1)


def _post_kernel(x_ref, o0_ref, o1_ref, o2_ref, l0_ref, l1_ref, l2_ref, yf_ref, yb_ref, u_ref,
                 g1_ref, dvec_ref, wglu_ref, bglu_ref, woa_ref, wos_ref, emat_ref, x1_ref,
                 to1_ref, to2_ref, tl1_ref, tl2_ref):
    dils = [dl for _, dl in DILATION_GROUPS]
    lses = [_natural_order(r, t, dl, LANES)
            for r, t, dl in zip((l0_ref, l1_ref, l2_ref), (None, tl1_ref, tl2_ref), dils)]
    outs = [_natural_order(r, t, dl, ATT_WIDTH)
            for r, t, dl in zip((o0_ref, o1_ref, o2_ref), (None, to1_ref, to2_ref), dils)]
    top = jnp.maximum(jnp.maximum(lses[0], lses[1]), lses[2])
    es = [jnp.exp(l - top) for l in lses]
    inv = 1.0 / (es[0] + es[1] + es[2])
    emat = emat_ref[...]
    att = None
    for e, o in zip(es, outs):
        term = _split_dot(e * inv, emat) * o
        att = term if att is None else att + term
    y = yf_ref[...] + yb_ref[...] + dvec_ref[...] * u_ref[...]
    y = 0.5 * y * (1.0 + jnp.tanh(math.sqrt(2.0 / math.pi) * (y + 0.044715 * (y * y * y))))
    z = jnp.dot(y.astype(BF16), wglu_ref[...], preferred_element_type=F32) + bglu_ref[...]
    ssm = y * jax.nn.sigmoid(z)
    mix = (jnp.dot(att.astype(BF16), woa_ref[...], preferred_element_type=F32)
           + jnp.dot(ssm.astype(BF16), wos_ref[...], preferred_element_type=F32))
    x1_ref[0] = x_ref[0] + g1_ref[0] * mix


def _post_mix(x, outs, lses, yf, yb, u_t, g1, prep):
    b, s, d = x.shape
    ts = min(512, s)
    tok = lambda w, dl=1: pl.BlockSpec((1, ts // dl, dl * w), lambda bi, si: (bi, si, 0))
    tmaj = pl.BlockSpec((ts, SSM_WIDTH), lambda bi, si: (si, bi))
    dils = [dl for _, dl in DILATION_GROUPS]
    return pl.pallas_call(
        _post_kernel,
        grid=(b, s // ts),
        in_specs=[tok(d)] + [tok(ATT_WIDTH, dl) for dl in dils] + [tok(LANES, dl) for dl in dils]
                 + [tmaj, tmaj, tmaj,
                  pl.BlockSpec((1, 1, d), lambda bi, si: (bi, 0, 0)),
                  _full((1, SSM_WIDTH)), _full((SSM_WIDTH, SSM_WIDTH)), _full((1, SSM_WIDTH)),
                  _full((ATT_WIDTH, d)), _full((SSM_WIDTH, d)), _full((LANES, ATT_WIDTH))],
        out_specs=tok(d),
        out_shape=jax.ShapeDtypeStruct((b, s, d), F32),
        scratch_shapes=([pltpu.VMEM((ATT_WIDTH // LANES, ts, LANES), F32)] * 2
                        + [pltpu.VMEM((1, ts, LANES), F32)] * 2),
        compiler_params=_cparams("parallel", "parallel"),
        name="post_mix",
    )(x, *outs, *lses, yf, yb, u_t, g1, prep["ssm_d"], prep["w_glu"], prep["b_glu"],
      prep["w_out_att"], prep["w_out_ssm"], prep["head_expand"])


RANK_BLOCK = 256


def _router_kernel(x1_ref, sc_ref, sh_ref, g_ref, wr_ref, br_ref, ltri_ref,
                   h2_ref, rcol_ref, rt_ref, gt_ref, cnt_ref, *, tt):
    x = x1_ref[...]
    ms = jnp.mean(x * x, axis=-1, keepdims=True)
    h = (x * lax.rsqrt(ms + EPS) * g_ref[...]) * (1.0 + sc_ref[0]) + sh_ref[0]
    h2_ref[...] = h.astype(BF16)
    lane = lax.broadcasted_iota(jnp.int32, (tt, LANES), 1)
    logits = jnp.dot(h, wr_ref[...], precision=HIGHEST, preferred_element_type=F32) + br_ref[...]
    work = jnp.where(lane < N_EXPERTS, logits, -jnp.inf)
    hots, vals = [], []
    for _ in range(TOP_K):
        mx = jnp.max(work, axis=1, keepdims=True)
        idx = jnp.min(jnp.where(work == mx, lane, LANES), axis=1, keepdims=True)
        hot = lane == idx
        hots.append(hot)
        vals.append(mx)
        work = jnp.where(hot, -jnp.inf, work)
    ws = [jnp.exp(v - vals[0]) for v in vals]
    inv = 1.0 / (ws[0] + ws[1] + ws[2] + ws[3])
    gates = jnp.zeros((tt, LANES), F32)
    sel = jnp.zeros((tt, LANES), F32)
    for hot, w in zip(hots, ws):
        gates = jnp.where(hot, w * inv, gates)
        sel = jnp.where(hot, 1.0, sel)
    ltri = ltri_ref[...]
    carry = jnp.zeros((1, LANES), F32)
    parts = []
    for blk in range(tt // RANK_BLOCK):
        sb = sel[blk * RANK_BLOCK:(blk + 1) * RANK_BLOCK]
        parts.append(jnp.dot(ltri, sb.astype(BF16), preferred_element_type=F32) + carry)
        carry = carry + jnp.sum(sb, axis=0, keepdims=True)
    rank = jnp.where(sel > 0.0, jnp.concatenate(parts, axis=0), -1.0)
    rcol_ref[...] = rank
    rt_ref[0] = rank.T[:N_EXPERTS]
    gt_ref[0] = gates.T[:N_EXPERTS]
    cnt_ref[0] = carry.astype(jnp.int32)


def _router(x1f, sc2, sh2, prep, tt, tiles_per_b):
    n, d = x1f.shape
    nt = n // tt
    vec = pl.BlockSpec((1, 1, d), lambda t: (t // tiles_per_b, 0, 0))
    kern = functools.partial(_router_kernel, tt=tt)
    return pl.pallas_call(
        kern,
        grid=(nt,),
        in_specs=[pl.BlockSpec((tt, d), lambda t: (t, 0)), vec, vec, _full((1, d)),
                  _full((d, LANES)), _full((1, LANES)), _full((RANK_BLOCK, RANK_BLOCK))],
        out_specs=[pl.BlockSpec((tt, d), lambda t: (t, 0)),
                   pl.BlockSpec((tt, LANES), lambda t: (t, 0)),
                   pl.BlockSpec((1, N_EXPERTS, tt), lambda t: (t, 0, 0)),
                   pl.BlockSpec((1, N_EXPERTS, tt), lambda t: (t, 0, 0)),
                   pl.BlockSpec((1, 1, LANES), lambda t: (t, 0, 0))],
        out_shape=[jax.ShapeDtypeStruct((n, d), BF16),
                   jax.ShapeDtypeStruct((n, LANES), F32),
                   jax.ShapeDtypeStruct((nt, N_EXPERTS, tt), F32),
                   jax.ShapeDtypeStruct((nt, N_EXPERTS, tt), F32),
                   jax.ShapeDtypeStruct((nt, 1, LANES), jnp.int32)],
        compiler_params=_cparams("parallel"),
        name="router",
    )(x1f, sc2, sh2, prep["norm2_g"], prep["w_router"], prep["b_router"], prep["ltri"])


def _moe_kernel(cnt_ref, h2_ref, x1_ref, rcol_ref, rt_ref, gt_ref, w1_ref, b1_ref, w2_ref, b2_ref,
                g2_ref, out_ref, *, tt, mc):
    t = pl.program_id(0)
    e = pl.program_id(1)

    @pl.when(e == 0)
    def _():
        out_ref[...] = x1_ref[...]

    n = cnt_ref[t * N_EXPERTS + e]
    lane = lax.broadcasted_iota(jnp.int32, (tt, LANES), 1)
    rsel = jnp.where(lane == e, rcol_ref[...], 0.0)
    rhi = rsel.astype(BF16)
    rlo = (rsel - rhi.astype(F32)).astype(BF16)
    mcp = -(-max(mc) // LANES) * LANES
    ones = jnp.ones((LANES, mcp), BF16)
    rbc = (jnp.dot(rhi, ones, preferred_element_type=F32)
           + jnp.dot(rlo, ones, preferred_element_type=F32))
    rrow = rt_ref[0, pl.ds(e, 1), :]
    grow = gt_ref[0, pl.ds(e, 1), :]

    start = 0
    for size in mc:
        base = float(start)

        @pl.when(start < n)
        def _(size=size, base=base):
            rcol = rbc[:, :size]
            ri = lax.broadcasted_iota(jnp.int32, (size, tt), 0).astype(F32) + base
            hit = rrow == ri
            xg = jnp.dot(jnp.where(hit, 1.0, 0.0).astype(BF16), h2_ref[...],
                         preferred_element_type=F32).astype(BF16)
            hh = jnp.dot(xg, w1_ref[0], preferred_element_type=F32) + b1_ref[0]
            glu = jnp.minimum(hh[:, :D_FF], SWIGLU_LIMIT)
            lin = jnp.clip(hh[:, D_FF:], -SWIGLU_LIMIT, SWIGLU_LIMIT)
            act = glu * jax.nn.sigmoid(SWIGLU_ALPHA * glu) * (lin + 1.0)
            y = jnp.dot(act.astype(BF16), w2_ref[0], preferred_element_type=F32) + b2_ref[0]
            gate = jnp.sum(jnp.where(hit, grow, 0.0), axis=1, keepdims=True)
            y = (y * gate) * g2_ref[0]
            ci = lax.broadcasted_iota(jnp.int32, (tt, size), 1).astype(F32) + base
            scat = jnp.where(rcol == ci, 1.0, 0.0).astype(BF16)
            out_ref[...] += jnp.dot(scat, y.astype(BF16), preferred_element_type=F32)

        start += size


def _moe(counts, h2, x1f, rcol, rt, gt, g2, prep, tt, mc, tiles_per_b):
    n, d = x1f.shape
    nt = n // tt
    kern = functools.partial(_moe_kernel, tt=tt, mc=mc)
    grid_spec = pltpu.PrefetchScalarGridSpec(
        num_scalar_prefetch=1,
        grid=(nt, N_EXPERTS),
        in_specs=[pl.BlockSpec((tt, d), lambda t, e, c: (t, 0), pipeline_mode=pl.Buffered(1)),
                  pl.BlockSpec((tt, d), lambda t, e, c: (t, 0), pipeline_mode=pl.Buffered(1)),
                  pl.BlockSpec((tt, LANES), lambda t, e, c: (t, 0), pipeline_mode=pl.Buffered(1)),
                  pl.BlockSpec((1, N_EXPERTS, tt), lambda t, e, c: (t, 0, 0)),
                  pl.BlockSpec((1, N_EXPERTS, tt), lambda t, e, c: (t, 0, 0)),
                  pl.BlockSpec((1, d, 2 * D_FF), lambda t, e, c: (e, 0, 0)),
                  pl.BlockSpec((1, 1, 2 * D_FF), lambda t, e, c: (e, 0, 0)),
                  pl.BlockSpec((1, D_FF, d), lambda t, e, c: (e, 0, 0)),
                  pl.BlockSpec((1, 1, d), lambda t, e, c: (e, 0, 0)),
                  pl.BlockSpec((1, 1, d), lambda t, e, c: (t // tiles_per_b, 0, 0))],
        out_specs=pl.BlockSpec((tt, d), lambda t, e, c: (t, 0)),
    )
    return pl.pallas_call(
        kern,
        grid_spec=grid_spec,
        out_shape=jax.ShapeDtypeStruct((n, d), F32),
        compiler_params=_cparams("parallel", "arbitrary"),
        name="sparse_moe",
    )(counts, h2, x1f, rcol, rt, gt, prep["w_mlp1"], prep["b_mlp1"], prep["w_mlp2"], prep["b_mlp2"], g2)


DEINT_BLOCK = 2 * LANES


def _deint_kernel(w_ref, perm_ref, o_ref):
    perm = perm_ref[...]
    half = DEINT_BLOCK // 2
    for blk in range(2 * D_FF // DEINT_BLOCK):
        wb = w_ref[0, :, blk * DEINT_BLOCK:(blk + 1) * DEINT_BLOCK].astype(BF16)
        res = jnp.dot(wb, perm, preferred_element_type=F32).astype(BF16)
        o_ref[0, :, blk * half:(blk + 1) * half] = res[:, :half]
        o_ref[0, :, D_FF + blk * half:D_FF + (blk + 1) * half] = res[:, half:]


def _deinterleave_mlp1(w_mlp1):
    idx = jnp.arange(DEINT_BLOCK)
    src = jnp.where(idx < DEINT_BLOCK // 2, 2 * idx, 2 * (idx - DEINT_BLOCK // 2) + 1)
    perm = (idx[:, None] == src[None, :]).astype(BF16)
    blk = (1, D_MODEL, 2 * D_FF)
    return pl.pallas_call(
        _deint_kernel,
        grid=(N_EXPERTS,),
        in_specs=[pl.BlockSpec(blk, lambda e: (e, 0, 0)), _full((DEINT_BLOCK, DEINT_BLOCK))],
        out_specs=pl.BlockSpec(blk, lambda e: (e, 0, 0)),
        out_shape=jax.ShapeDtypeStruct(w_mlp1.shape, BF16),
        compiler_params=_cparams("parallel"),
        name="mlp1_regroup",
    )(w_mlp1, perm)


def _ssm_discretise(a_re, a_im, log_step, b_re, b_im):
    a_re = jnp.minimum(a_re, -SSM_MIN_DECAY)
    dt = jnp.exp(log_step)[:, None]
    mag = jnp.exp(a_re * dt)
    ab_re = mag * jnp.cos(a_im * dt)
    ab_im = mag * jnp.sin(a_im * dt)
    den = a_re * a_re + a_im * a_im
    xr = ab_re - 1.0
    z_re = (xr * a_re + ab_im * a_im) / den
    z_im = (ab_im * a_re - xr * a_im) / den
    bb_re = z_re[..., None] * b_re - z_im[..., None] * b_im
    bb_im = z_re[..., None] * b_im + z_im[..., None] * b_re
    return ab_re, ab_im, bb_re, bb_im


def _prepare(max_s, norm1_g, norm2_g, w_in, q_norm_g, k_norm_g, ssm_a_re, ssm_a_im, ssm_log_step,
             ssm_b_re, ssm_b_im, ssm_c_re, ssm_c_im, ssm_d, w_glu, b_glu, w_out, w_router, b_router,
             w_mlp1, b_mlp1, w_mlp2, b_mlp2):
    g, p, c = SSM_GROUPS, SSM_STATE, SSM_GROUP_CH
    eye = jnp.eye(g, dtype=F32)
    a_l, b_l, c_l = [], [], []
    for dr in range(2):
        ab_re, ab_im, bb_re, bb_im = _ssm_discretise(ssm_a_re[dr], ssm_a_im[dr], ssm_log_step[dr],
                                                     ssm_b_re[dr], ssm_b_im[dr])
        a_l.append(jnp.stack([ab_re.reshape(g * p), ab_im.reshape(g * p)]))
        bd = lambda blk: jnp.einsum("gpc,gh->gchp", blk, eye).reshape(g * c, g * p)
        b_l.append(jnp.concatenate([bd(bb_re), bd(bb_im)], axis=1).astype(BF16))
        cd = lambda blk: jnp.einsum("gcp,gh->gphc", blk, eye).reshape(g * p, g * c)
        c_l.append(jnp.concatenate([cd(ssm_c_re[dr]), -cd(ssm_c_im[dr])], axis=0).astype(BF16))

    pos = jnp.arange(max_s, dtype=F32)
    inv = 1.0 / (ROPE_THETA ** (jnp.arange(0, ROT_DIM, 2, dtype=F32) / ROT_DIM))
    ang = pos[:, None] * inv[None, :]
    cos, sin = jnp.cos(ang), jnp.sin(ang)
    half = ROT_DIM // 2
    rest = HEAD_DIM - ROT_DIM
    zeros = lambda w: jnp.zeros((max_s, w), F32)
    per_head = lambda t: jnp.tile(t, (1, LANES // HEAD_DIM))
    rope_c = per_head(jnp.concatenate([cos, cos, jnp.ones((max_s, rest), F32)], axis=1))
    rope_s1 = per_head(jnp.concatenate([-sin, zeros(half + rest)], axis=1))
    rope_s2 = per_head(jnp.concatenate([zeros(half), sin, zeros(rest)], axis=1))

    idx = jnp.arange(2 * LANES)
    head_ind = (idx[:, None] // HEAD_DIM == idx[None, :] // HEAD_DIM).astype(BF16)
    head_expand = (jnp.arange(LANES)[:, None] == jnp.arange(ATT_WIDTH)[None, :] // HEAD_DIM).astype(BF16)
    ridx = jnp.arange(RANK_BLOCK)
    ltri = (ridx[None, :] < ridx[:, None]).astype(BF16)

    return dict(
        norm1_g=norm1_g.reshape(1, D_MODEL), norm2_g=norm2_g.reshape(1, D_MODEL),
        w_in=w_in.astype(BF16),
        q_gain=jnp.tile(q_norm_g, (1, ATT_HEADS)) * (HEAD_DIM ** -0.5),
        k_gain=jnp.tile(k_norm_g, (1, ATT_HEADS)),
        rope_c=rope_c, rope_s1=rope_s1, rope_s2=rope_s2, head_ind=head_ind, head_expand=head_expand,
        ltri=ltri, ssm_a=a_l, ssm_b=b_l, ssm_c=c_l,
        ssm_d=ssm_d.reshape(1, SSM_WIDTH), w_glu=w_glu.astype(BF16), b_glu=b_glu.reshape(1, SSM_WIDTH),
        w_out_att=w_out[:ATT_WIDTH].astype(BF16), w_out_ssm=w_out[ATT_WIDTH:].astype(BF16),
        w_router=jnp.pad(w_router, ((0, 0), (0, LANES - N_EXPERTS))),
        b_router=jnp.pad(b_router, (0, LANES - N_EXPERTS)).reshape(1, LANES),
        w_mlp1=_deinterleave_mlp1(w_mlp1),
        b_mlp1=jnp.concatenate([b_mlp1[..., 0::2], b_mlp1[..., 1::2]], axis=-1).reshape(N_EXPERTS, 1, 2 * D_FF),
        w_mlp2=w_mlp2.astype(BF16), b_mlp2=b_mlp2.reshape(N_EXPERTS, 1, D_MODEL),
    )


def _moe_tiles(s):
    tt = min(2048, s)
    first = max(16, tt // 8)
    spill = max(16, tt // 32)
    sizes = [first, spill]
    while sum(sizes) < tt:
        sizes.append(min(first, tt - sum(sizes)))
    return tt, tuple(sizes)


def _encoder_layer(x, mod, prep):
    b, s, d = x.shape
    sh1, sc1, g1, sh2, sc2, g2 = [m.reshape(b, 1, d) for m in jnp.split(mod, N_MOD, axis=-1)]
    qkv0, qkv1, qkv2, u_t = _in_projection(x, sc1, sh1, prep)
    outs, lses = [], []
    for qkv, (_, dil) in zip((qkv0, qkv1, qkv2), DILATION_GROUPS):
        o, lse = _band_attention(qkv, dil)
        outs.append(o)
        lses.append(lse)
    yf, yb = _ssm_scan(u_t, prep, b)
    x1 = _post_mix(x, outs, lses, yf, yb, u_t, g1, prep)
    tt, mc = _moe_tiles(s)
    x1f = x1.reshape(b * s, d)
    h2, rcol, rt, gt, cnt = _router(x1f, sc2, sh2, prep, tt, s // tt)
    counts = cnt[:, 0, :N_EXPERTS].reshape(-1)
    out = _moe(counts, h2, x1f, rcol, rt, gt, g2, prep, tt, mc, s // tt)
    return out.reshape(b, s, d)


def kernel(x_prompt, x_sample, c_prompt, c_sample, w_ada, b_ada, norm1_g, norm2_g, w_in, q_norm_g, k_norm_g, ssm_a_re, ssm_a_im, ssm_log_step, ssm_b_re, ssm_b_im, ssm_c_re, ssm_c_im, ssm_d, w_glu, b_glu, w_out, w_router, b_router, w_mlp1, b_mlp1, w_mlp2, b_mlp2):
    depth = w_ada.shape[0]
    y_prompt, y_sample = x_prompt, x_sample
    nbp = x_prompt.shape[0]
    max_s = max(x_prompt.shape[1], x_sample.shape[1])
    for l in range(depth):
        prep = _prepare(max_s, norm1_g[l], norm2_g[l], w_in[l], q_norm_g[l], k_norm_g[l], ssm_a_re[l],
                        ssm_a_im[l], ssm_log_step[l], ssm_b_re[l], ssm_b_im[l], ssm_c_re[l], ssm_c_im[l],
                        ssm_d[l], w_glu[l], b_glu[l], w_out[l], w_router[l], b_router[l], w_mlp1[l],
                        b_mlp1[l], w_mlp2[l], b_mlp2[l])
        mod = _modulation(jnp.concatenate([c_prompt, c_sample], axis=0), w_ada[l], b_ada[l])
        y_prompt = _encoder_layer(y_prompt, mod[:nbp], prep)
        y_sample = _encoder_layer(y_sample, mod[nbp:], prep)
    return (y_prompt, y_sample)
```

```python
import functools
import math

import jax
import jax.numpy as jnp
from jax import lax
from jax.experimental import pallas as pl
from jax.experimental.pallas import tpu as pltpu

F32 = jnp.float32
BF16 = jnp.bfloat16
HIGHEST = lax.Precision.HIGHEST

D_MODEL = 1024
HEAD_DIM = 64
ATT_HEADS = 8
DILATION_GROUPS = ((128, 1), (512, 4), (2048, 16))
N_DIL = len(DILATION_GROUPS)
ATT_WIDTH = ATT_HEADS * HEAD_DIM
QKV_WIDTH = 3 * ATT_WIDTH
BAND_RADIUS = 64
ROT_DIM = HEAD_DIM // 4
ROPE_THETA = 500000.0
SSM_GROUP_CH = 16
SSM_GROUPS = 16
SSM_WIDTH = SSM_GROUPS * SSM_GROUP_CH
SSM_STATE = 64
SSM_LANES = SSM_GROUPS * SSM_STATE
SSM_MIN_DECAY = 1e-4
ATT_COLS = N_DIL * QKV_WIDTH
N_EXPERTS = 32
TOP_K = 4
D_FF = D_MODEL
SWIGLU_LIMIT = 7.0
SWIGLU_ALPHA = 1.702
N_MOD = 6
EPS = 1e-6
MASK_VALUE = -1e30

LANES = 128
VMEM_LIMIT = 56 * 1024 * 1024


def _cparams(*sem):
    return pltpu.CompilerParams(dimension_semantics=sem, vmem_limit_bytes=VMEM_LIMIT)


def _full(shape):
    n = len(shape)
    return pl.BlockSpec(shape, lambda *_: (0,) * n)


def _full_once(shape):
    n = len(shape)
    return pl.BlockSpec(shape, lambda *_: (0,) * n, pipeline_mode=pl.Buffered(1))


def _mod_kernel(c_ref, w_ref, b_ref, o_ref):
    c = c_ref[...]
    a = c * jax.nn.sigmoid(c)
    o_ref[...] = jnp.dot(a, w_ref[...], precision=HIGHEST, preferred_element_type=F32) + b_ref[...]


def _modulation(c_all, w_ada, b_ada):
    nb = c_all.shape[0]
    ncol = w_ada.shape[1]
    bw = D_MODEL
    return pl.pallas_call(
        _mod_kernel,
        grid=(ncol // bw,),
        in_specs=[_full((nb, D_MODEL)),
                  pl.BlockSpec((D_MODEL, bw), lambda j: (0, j)),
                  pl.BlockSpec((1, bw), lambda j: (0, j))],
        out_specs=pl.BlockSpec((nb, bw), lambda j: (0, j)),
        out_shape=jax.ShapeDtypeStruct((nb, ncol), F32),
        compiler_params=_cparams("arbitrary"),
        name="modulation",
    )(c_all, w_ada, b_ada.reshape(1, ncol))


def _head_rms(t, gmat):
    sq = (t * t).astype(BF16)
    half = 2 * LANES
    ss = jnp.concatenate(
        [jnp.dot(sq[:, :half], gmat, preferred_element_type=F32),
         jnp.dot(sq[:, half:], gmat, preferred_element_type=F32)], axis=1)
    return t * lax.rsqrt(ss * (1.0 / HEAD_DIM) + EPS)


def _rope(t, cs, s1, s2):
    return (t * cs + pltpu.roll(t, ATT_WIDTH - ROT_DIM // 2, 1) * s1
            + pltpu.roll(t, ROT_DIM // 2, 1) * s2)


def _store_dilated(o_ref, tmp_ref, col, val, dil):
    if dil == 1:
        o_ref[0, :, col:col + ATT_WIDTH] = val.astype(BF16)
        return
    rows = val.shape[0] // dil
    nchunk = ATT_WIDTH // LANES
    for c in range(nchunk):
        tmp_ref[c] = val[:, c * LANES:(c + 1) * LANES]
    for r in range(dil):
        c0 = r * QKV_WIDTH + col
        picked = [tmp_ref[c, pl.ds(r, rows, stride=dil), :] for c in range(nchunk)]
        o_ref[0, :, c0:c0 + ATT_WIDTH] = jnp.concatenate(picked, axis=1).astype(BF16)


def _proj_kernel(x_ref, sc_ref, sh_ref, g_ref, w_ref, qg_ref, kg_ref, cs_ref, s1_ref, s2_ref, gmat_ref,
                 o0_ref, o1_ref, o2_ref, u_ref, tq_ref, tk_ref, tv_ref):
    x = x_ref[0]
    ms = jnp.mean(x * x, axis=-1, keepdims=True)
    h = (x * lax.rsqrt(ms + EPS) * g_ref[...]) * (1.0 + sc_ref[0]) + sh_ref[0]
    hb = h.astype(BF16)
    rep = ATT_WIDTH // LANES
    cs = jnp.concatenate([cs_ref[...]] * rep, axis=1)
    s1 = jnp.concatenate([s1_ref[...]] * rep, axis=1)
    s2 = jnp.concatenate([s2_ref[...]] * rep, axis=1)
    gmat = gmat_ref[...]
    for g, o_ref in enumerate((o0_ref, o1_ref, o2_ref)):
        p = jnp.dot(hb, w_ref[:, g * QKV_WIDTH:(g + 1) * QKV_WIDTH], preferred_element_type=F32)
        q = _rope(_head_rms(p[:, :ATT_WIDTH], gmat) * qg_ref[g:g + 1, :], cs, s1, s2)
        k = _rope(_head_rms(p[:, ATT_WIDTH:2 * ATT_WIDTH], gmat) * kg_ref[g:g + 1, :], cs, s1, s2)
        dil = DILATION_GROUPS[g][1]
        _store_dilated(o_ref, tq_ref, 0, q, dil)
        _store_dilated(o_ref, tk_ref, ATT_WIDTH, k, dil)
        _store_dilated(o_ref, tv_ref, 2 * ATT_WIDTH, p[:, 2 * ATT_WIDTH:], dil)
    u_ref[...] = jnp.dot(hb, w_ref[:, ATT_COLS:], preferred_element_type=F32)


def _in_projection(x, sc1, sh1, prep):
    b, s, d = x.shape
    ts = min(512, s)
    qkv_shape = [jax.ShapeDtypeStruct((b, s // dl, dl * QKV_WIDTH), BF16) for _, dl in DILATION_GROUPS]
    qkv_spec = [pl.BlockSpec((1, ts // dl, dl * QKV_WIDTH), lambda bi, si: (bi, si, 0))
                for _, dl in DILATION_GROUPS]
    vec_spec = pl.BlockSpec((1, 1, d), lambda bi, si: (bi, 0, 0))
    tab_spec = pl.BlockSpec((ts, LANES), lambda bi, si: (si, 0))
    return pl.pallas_call(
        _proj_kernel,
        grid=(b, s // ts),
        in_specs=[pl.BlockSpec((1, ts, d), lambda bi, si: (bi, si, 0)), vec_spec, vec_spec,
                  _full((1, d)), _full_once(prep["w_in"].shape),
                  _full((N_DIL, ATT_WIDTH)), _full((N_DIL, ATT_WIDTH)),
                  tab_spec, tab_spec, tab_spec, _full((2 * LANES, 2 * LANES))],
        out_specs=qkv_spec + [pl.BlockSpec((ts, SSM_WIDTH), lambda bi, si: (si, bi))],
        out_shape=qkv_shape + [jax.ShapeDtypeStruct((s, b * SSM_WIDTH), F32)],
        scratch_shapes=[pltpu.VMEM((ATT_WIDTH // LANES, ts, LANES), F32)] * 3,
        compiler_params=_cparams("parallel", "parallel"),
        name="in_projection",
    )(x, sc1, sh1, prep["norm1_g"], prep["w_in"], prep["q_gain"], prep["k_gain"],
      prep["rope_c"][:s], prep["rope_s1"][:s], prep["rope_s2"][:s], prep["head_ind"])


ATTN_UNROLL = 4


def _attn_kernel(qkv_ref, o_ref, lse_ref, *, m, qb, win, rp):
    lane = lax.broadcasted_iota(jnp.int32, (qb, LANES), 1)
    lower = lane < HEAD_DIM
    row_i = lax.broadcasted_iota(jnp.int32, (qb, win), 0)
    col_i = lax.broadcasted_iota(jnp.int32, (qb, win), 1)

    for rr in range(rp):
        base = rr * QKV_WIDTH

        def block(j, carry, base=base, rr=rr):
            q0 = pl.multiple_of(j * qb, qb)
            ws = jnp.clip(q0 - BAND_RADIUS, 0, m - win)
            ws = pl.multiple_of(ws, math.gcd(BAND_RADIUS, win))
            valid = jnp.abs((ws + col_i) - (q0 + row_i)) <= BAND_RADIUS
            lse_blk = jnp.zeros((qb, LANES), F32)
            for pair in range(ATT_WIDTH // LANES):
                off = base + pair * LANES
                q2 = qkv_ref[0, pl.ds(q0, qb), off:off + LANES]
                k2 = qkv_ref[0, pl.ds(ws, win), ATT_WIDTH + off:ATT_WIDTH + off + LANES]
                v2 = qkv_ref[0, pl.ds(ws, win), 2 * ATT_WIDTH + off:2 * ATT_WIDTH + off + LANES]
                outs = []
                for a in range(2):
                    head_lanes = lower if a == 0 else jnp.logical_not(lower)
                    qa = jnp.where(head_lanes, q2, jnp.zeros_like(q2))
                    sc = lax.dot_general(qa, k2, (((1,), (1,)), ((), ())), preferred_element_type=F32)
                    sc = jnp.where(valid, sc, MASK_VALUE)
                    mx = jnp.max(sc, axis=1, keepdims=True)
                    p = jnp.exp(sc - mx)
                    den = jnp.sum(p, axis=1, keepdims=True)
                    pv = jnp.dot(p.astype(BF16), v2, preferred_element_type=F32)
                    outs.append(pv * (1.0 / den))
                    lse_blk = jnp.where(lane == 2 * pair + a, mx + jnp.log(den), lse_blk)
                oc = rr * ATT_WIDTH + pair * LANES
                o_ref[0, pl.ds(q0, qb), oc:oc + LANES] = jnp.where(lower, outs[0], outs[1]).astype(BF16)
            lse_ref[0, pl.ds(q0, qb), rr * LANES:(rr + 1) * LANES] = lse_blk
            return carry

        nblk = m // qb
        lax.fori_loop(0, nblk, block, 0, unroll=min(ATTN_UNROLL, nblk))


def _band_attention(view, dil):
    b, m, _ = view.shape
    qb = min(128, m)
    win = min(qb + 2 * BAND_RADIUS, m)
    rp = max(1, min(dil, ATTN_UNROLL * qb // m))
    kern = functools.partial(_attn_kernel, m=m, qb=qb, win=win, rp=rp)
    return pl.pallas_call(
        kern,
        grid=(b, dil // rp),
        in_specs=[pl.BlockSpec((1, m, rp * QKV_WIDTH), lambda bi, r: (bi, 0, r))],
        out_specs=[pl.BlockSpec((1, m, rp * ATT_WIDTH), lambda bi, r: (bi, 0, r)),
                   pl.BlockSpec((1, m, rp * LANES), lambda bi, r: (bi, 0, r))],
        out_shape=[jax.ShapeDtypeStruct((b, m, dil * ATT_WIDTH), BF16),
                   jax.ShapeDtypeStruct((b, m, dil * LANES), F32)],
        compiler_params=_cparams("parallel", "parallel"),
        name=f"band_attention_d{dil}",
    )(view)


SCAN_LANES = 256


def _ssm_kernel(uf_ref, ub_ref, bf_ref, bb_ref, cf_ref, cb_ref, af_ref, ab_ref, yf_ref, yb_ref,
                hf_s, hb_s, carf_s, carb_s, *, nb, tc):
    rows = nb * tc

    @pl.when(pl.program_id(0) == 0)
    def _():
        carf_s[...] = jnp.zeros_like(carf_s)
        carb_s[...] = jnp.zeros_like(carb_s)

    hf_s[...] = jnp.dot(uf_ref[...].reshape(rows, SSM_WIDTH).astype(BF16), bf_ref[...],
                        preferred_element_type=F32)
    hb_s[...] = jnp.dot(ub_ref[...].reshape(rows, SSM_WIDTH).astype(BF16), bb_ref[...],
                        preferred_element_type=F32)

    for c in range(SSM_LANES // SCAN_LANES):
        re = slice(c * SCAN_LANES, (c + 1) * SCAN_LANES)
        im = slice(SSM_LANES + c * SCAN_LANES, SSM_LANES + (c + 1) * SCAN_LANES)
        far = jnp.broadcast_to(af_ref[0:1, re], (nb, SCAN_LANES))
        fai = jnp.broadcast_to(af_ref[1:2, re], (nb, SCAN_LANES))
        bar = jnp.broadcast_to(ab_ref[0:1, re], (nb, SCAN_LANES))
        bai = jnp.broadcast_to(ab_ref[1:2, re], (nb, SCAN_LANES))

        def step(t, carry):
            fr, fi, br, bi = carry
            rf = pl.ds(pl.multiple_of(t * nb, nb), nb)
            rb = pl.ds(pl.multiple_of((tc - 1 - t) * nb, nb), nb)
            nfr = far * fr - fai * fi + hf_s[rf, re]
            nfi = far * fi + fai * fr + hf_s[rf, im]
            nbr = bar * br - bai * bi + hb_s[rb, re]
            nbi = bar * bi + bai * br + hb_s[rb, im]
            hf_s[rf, re] = nfr
            hf_s[rf, im] = nfi
            hb_s[rb, re] = nbr
            hb_s[rb, im] = nbi
            return nfr, nfi, nbr, nbi

        fr, fi, br, bi = lax.fori_loop(
            0, tc, step, (carf_s[:, re], carf_s[:, im], carb_s[:, re], carb_s[:, im]), unroll=2)
        carf_s[:, re] = fr
        carf_s[:, im] = fi
        carb_s[:, re] = br
        carb_s[:, im] = bi

    yf_ref[...] = jnp.dot(hf_s[...].astype(BF16), cf_ref[...],
                          preferred_element_type=F32).reshape(tc, nb, SSM_WIDTH)
    yb_ref[...] = jnp.dot(hb_s[...].astype(BF16), cb_ref[...],
                          preferred_element_type=F32).reshape(tc, nb, SSM_WIDTH)


def _ssm_scan(u_t, prep, nb):
    s = u_t.shape[0]
    u3 = u_t.reshape(s, nb, SSM_WIDTH)
    tc = min(1024 // nb, s)
    nchunk = s // tc
    kern = functools.partial(_ssm_kernel, nb=nb, tc=tc)
    blk = (tc, nb, SSM_WIDTH)
    yf, yb = pl.pallas_call(
        kern,
        grid=(nchunk,),
        in_specs=[pl.BlockSpec(blk, lambda i: (i, 0, 0)),
                  pl.BlockSpec(blk, lambda i: (nchunk - 1 - i, 0, 0)),
                  _full((SSM_WIDTH, 2 * SSM_LANES)), _full((SSM_WIDTH, 2 * SSM_LANES)),
                  _full((2 * SSM_LANES, SSM_WIDTH)), _full((2 * SSM_LANES, SSM_WIDTH)),
                  _full((2, SSM_LANES)), _full((2, SSM_LANES))],
        out_specs=[pl.BlockSpec(blk, lambda i: (i, 0, 0)),
                   pl.BlockSpec(blk, lambda i: (nchunk - 1 - i, 0, 0))],
        out_shape=[jax.ShapeDtypeStruct((s, nb, SSM_WIDTH), F32)] * 2,
        scratch_shapes=[pltpu.VMEM((nb * tc, 2 * SSM_LANES), F32),
                        pltpu.VMEM((nb * tc, 2 * SSM_LANES), F32),
                        pltpu.VMEM((nb, 2 * SSM_LANES), F32),
                        pltpu.VMEM((nb, 2 * SSM_LANES), F32)],
        compiler_params=_cparams("arbitrary"),
        name="s5_scan",
    )(u3, u3, prep["ssm_b"][0], prep["ssm_b"][1], prep["ssm_c"][0], prep["ssm_c"][1],
      prep["ssm_a"][0], prep["ssm_a"][1])
    return yf.reshape(s, nb * SSM_WIDTH), yb.reshape(s, nb * SSM_WIDTH)


def _split_dot(w, mat):
    hi = w.astype(BF16)
    lo = (w - hi.astype(F32)).astype(BF16)
    return (jnp.dot(hi, mat, preferred_element_type=F32)
            + jnp.dot(lo, mat, preferred_element_type=F32))


def _natural_order(ref, tmp_ref, dil, width):
    if dil == 1:
        return ref[0].astype(F32)
    rows = ref.shape[1]
    nchunk = width // LANES
    for r in range(dil):
        blk = ref[0, :, r * width:(r + 1) * width].astype(F32)
        for c in range(nchunk):
            tmp_ref[c, pl.ds(r, rows, stride=dil), :] = blk[:, c * LANES:(c + 1) * LANES]
    return jnp.concatenate([tmp_ref[c] for c in range(nchunk)], axis=1)


def _post_kernel(x_ref, o0_ref, o1_ref, o2_ref, l0_ref, l1_ref, l2_ref, yf_ref, yb_ref, u_ref,
                 g1_ref, dvec_ref, wglu_ref, bglu_ref, woa_ref, wos_ref, emat_ref, x1_ref,
                 to1_ref, to2_ref, tl1_ref, tl2_ref):
    dils = [dl for _, dl in DILATION_GROUPS]
    lses = [_natural_order(r, t, dl, LANES)
            for r, t, dl in zip((l0_ref, l1_ref, l2_ref), (None, tl1_ref, tl2_ref), dils)]
    outs = [_natural_order(r, t, dl, ATT_WIDTH)
            for r, t, dl in zip((o0_ref, o1_ref, o2_ref), (None, to1_ref, to2_ref), dils)]
    top = jnp.maximum(jnp.maximum(lses[0], lses[1]), lses[2])
    es = [jnp.exp(l - top) for l in lses]
    inv = 1.0 / (es[0] + es[1] + es[2])
    emat = emat_ref[...]
    att = None
    for e, o in zip(es, outs):
        term = _split_dot(e * inv, emat) * o
        att = term if att is None else att + term
    y = yf_ref[...] + yb_ref[...] + dvec_ref[...] * u_ref[...]
    y = 0.5 * y * (1.0 + jnp.tanh(math.sqrt(2.0 / math.pi) * (y + 0.044715 * (y * y * y))))
    z = jnp.dot(y.astype(BF16), wglu_ref[...], preferred_element_type=F32) + bglu_ref[...]
    ssm = y * jax.nn.sigmoid(z)
    mix = (jnp.dot(att.astype(BF16), woa_ref[...], preferred_element_type=F32)
           + jnp.dot(ssm.astype(BF16), wos_ref[...], preferred_element_type=F32))
    x1_ref[0] = x_ref[0] + g1_ref[0] * mix


def _post_mix(x, outs, lses, yf, yb, u_t, g1, prep):
    b, s, d = x.shape
    ts = min(512, s)
    tok = lambda w, dl=1: pl.BlockSpec((1, ts // dl, dl * w), lambda bi, si: (bi, si, 0))
    tmaj = pl.BlockSpec((ts, SSM_WIDTH), lambda bi, si: (si, bi))
    dils = [dl for _, dl in DILATION_GROUPS]
    return pl.pallas_call(
        _post_kernel,
        grid=(b, s // ts),
        in_specs=[tok(d)] + [tok(ATT_WIDTH, dl) for dl in dils] + [tok(LANES, dl) for dl in dils]
                 + [tmaj, tmaj, tmaj,
                  pl.BlockSpec((1, 1, d), lambda bi, si: (bi, 0, 0)),
                  _full((1, SSM_WIDTH)), _full((SSM_WIDTH, SSM_WIDTH)), _full((1, SSM_WIDTH)),
                  _full((ATT_WIDTH, d)), _full((SSM_WIDTH, d)), _full((LANES, ATT_WIDTH))],
        out_specs=tok(d),
        out_shape=jax.ShapeDtypeStruct((b, s, d), F32),
        scratch_shapes=([pltpu.VMEM((ATT_WIDTH // LANES, ts, LANES), F32)] * 2
                        + [pltpu.VMEM((1, ts, LANES), F32)] * 2),
        compiler_params=_cparams("parallel", "parallel"),
        name="post_mix",
    )(x, *outs, *lses, yf, yb, u_t, g1, prep["ssm_d"], prep["w_glu"], prep["b_glu"],
      prep["w_out_att"], prep["w_out_ssm"], prep["head_expand"])


RANK_BLOCK = 256


def _router_kernel(x1_ref, sc_ref, sh_ref, g_ref, wr_ref, br_ref, ltri_ref,
                   h2_ref, rcol_ref, rt_ref, gt_ref, cnt_ref, *, tt):
    x = x1_ref[...]
    ms = jnp.mean(x * x, axis=-1, keepdims=True)
    h = (x * lax.rsqrt(ms + EPS) * g_ref[...]) * (1.0 + sc_ref[0]) + sh_ref[0]
    h2_ref[...] = h.astype(BF16)
    lane = lax.broadcasted_iota(jnp.int32, (tt, LANES), 1)
    logits = jnp.dot(h, wr_ref[...], precision=HIGHEST, preferred_element_type=F32) + br_ref[...]
    work = jnp.where(lane < N_EXPERTS, logits, -jnp.inf)
    hots, vals = [], []
    for _ in range(TOP_K):
        mx = jnp.max(work, axis=1, keepdims=True)
        idx = jnp.min(jnp.where(work == mx, lane, LANES), axis=1, keepdims=True)
        hot = lane == idx
        hots.append(hot)
        vals.append(mx)
        work = jnp.where(hot, -jnp.inf, work)
    ws = [jnp.exp(v - vals[0]) for v in vals]
    inv = 1.0 / (ws[0] + ws[1] + ws[2] + ws[3])
    gates = jnp.zeros((tt, LANES), F32)
    sel = jnp.zeros((tt, LANES), F32)
    for hot, w in zip(hots, ws):
        gates = jnp.where(hot, w * inv, gates)
        sel = jnp.where(hot, 1.0, sel)
    ltri = ltri_ref[...]
    carry = jnp.zeros((1, LANES), F32)
    parts = []
    for blk in range(tt // RANK_BLOCK):
        sb = sel[blk * RANK_BLOCK:(blk + 1) * RANK_BLOCK]
        parts.append(jnp.dot(ltri, sb.astype(BF16), preferred_element_type=F32) + carry)
        carry = carry + jnp.sum(sb, axis=0, keepdims=True)
    rank = jnp.where(sel > 0.0, jnp.concatenate(parts, axis=0), -1.0)
    rcol_ref[...] = rank
    rt_ref[0] = rank.T[:N_EXPERTS]
    gt_ref[0] = gates.T[:N_EXPERTS]
    cnt_ref[0] = carry.astype(jnp.int32)


def _router(x1f, sc2, sh2, prep, tt, tiles_per_b):
    n, d = x1f.shape
    nt = n // tt
    vec = pl.BlockSpec((1, 1, d), lambda t: (t // tiles_per_b, 0, 0))
    kern = functools.partial(_router_kernel, tt=tt)
    return pl.pallas_call(
        kern,
        grid=(nt,),
        in_specs=[pl.BlockSpec((tt, d), lambda t: (t, 0)), vec, vec, _full((1, d)),
                  _full((d, LANES)), _full((1, LANES)), _full((RANK_BLOCK, RANK_BLOCK))],
        out_specs=[pl.BlockSpec((tt, d), lambda t: (t, 0)),
                   pl.BlockSpec((tt, LANES), lambda t: (t, 0)),
                   pl.BlockSpec((1, N_EXPERTS, tt), lambda t: (t, 0, 0)),
                   pl.BlockSpec((1, N_EXPERTS, tt), lambda t: (t, 0, 0)),
                   pl.BlockSpec((1, 1, LANES), lambda t: (t, 0, 0))],
        out_shape=[jax.ShapeDtypeStruct((n, d), BF16),
                   jax.ShapeDtypeStruct((n, LANES), F32),
                   jax.ShapeDtypeStruct((nt, N_EXPERTS, tt), F32),
                   jax.ShapeDtypeStruct((nt, N_EXPERTS, tt), F32),
                   jax.ShapeDtypeStruct((nt, 1, LANES), jnp.int32)],
        compiler_params=_cparams("parallel"),
        name="router",
    )(x1f, sc2, sh2, prep["norm2_g"], prep["w_router"], prep["b_router"], prep["ltri"])


def _moe_kernel(cnt_ref, h2_ref, x1_ref, rcol_ref, rt_ref, gt_ref, w1_ref, b1_ref, w2_ref, b2_ref,
                g2_ref, out_ref, *, tt, mc, grp):
    t = pl.program_id(0)
    e = pl.program_id(1)

    @pl.when(e == 0)
    def _():
        out_ref[...] = x1_ref[...]

    lane = lax.broadcasted_iota(jnp.int32, (tt, LANES), 1)
    mcp = -(-max(mc) // LANES) * LANES
    ones = jnp.ones((LANES, mcp), BF16)

    for sub in range(grp):
        rows = slice(sub * tt, (sub + 1) * tt)
        n = cnt_ref[(t * grp + sub) * N_EXPERTS + e]
        rsel = jnp.where(lane == e, rcol_ref[rows, :], 0.0)
        rhi = rsel.astype(BF16)
        rlo = (rsel - rhi.astype(F32)).astype(BF16)
        rbc = (jnp.dot(rhi, ones, preferred_element_type=F32)
               + jnp.dot(rlo, ones, preferred_element_type=F32))
        rrow = rt_ref[sub, pl.ds(e, 1), :]
        grow = gt_ref[sub, pl.ds(e, 1), :]

        start = 0
        for size in mc:
            base = float(start)

            @pl.when(start < n)
            def _(size=size, base=base, rows=rows, rbc=rbc, rrow=rrow, grow=grow):
                rcol = rbc[:, :size]
                ri = lax.broadcasted_iota(jnp.int32, (size, tt), 0).astype(F32) + base
                hit = rrow == ri
                xg = jnp.dot(jnp.where(hit, 1.0, 0.0).astype(BF16), h2_ref[rows, :],
                             preferred_element_type=F32).astype(BF16)
                hh = jnp.dot(xg, w1_ref[0], preferred_element_type=F32) + b1_ref[0]
                glu = jnp.minimum(hh[:, :D_FF], SWIGLU_LIMIT)
                lin = jnp.clip(hh[:, D_FF:], -SWIGLU_LIMIT, SWIGLU_LIMIT)
                act = glu * jax.nn.sigmoid(SWIGLU_ALPHA * glu) * (lin + 1.0)
                y = jnp.dot(act.astype(BF16), w2_ref[0], preferred_element_type=F32) + b2_ref[0]
                gate = jnp.sum(jnp.where(hit, grow, 0.0), axis=1, keepdims=True)
                y = (y * gate) * g2_ref[0]
                ci = lax.broadcasted_iota(jnp.int32, (tt, size), 1).astype(F32) + base
                scat = jnp.where(rcol == ci, 1.0, 0.0).astype(BF16)
                out_ref[rows, :] += jnp.dot(scat, y.astype(BF16), preferred_element_type=F32)

            start += size


def _moe(counts, h2, x1f, rcol, rt, gt, g2, prep, tt, mc, grp, tiles_per_b):
    n, d = x1f.shape
    nt = n // (tt * grp)
    kern = functools.partial(_moe_kernel, tt=tt, mc=mc, grp=grp)
    once = pl.Buffered(1)
    grid_spec = pltpu.PrefetchScalarGridSpec(
        num_scalar_prefetch=1,
        grid=(nt, N_EXPERTS),
        in_specs=[pl.BlockSpec((grp * tt, d), lambda t, e, c: (t, 0), pipeline_mode=once),
                  pl.BlockSpec((grp * tt, d), lambda t, e, c: (t, 0), pipeline_mode=once),
                  pl.BlockSpec((grp * tt, LANES), lambda t, e, c: (t, 0), pipeline_mode=once),
                  pl.BlockSpec((grp, N_EXPERTS, tt), lambda t, e, c: (t, 0, 0)),
                  pl.BlockSpec((grp, N_EXPERTS, tt), lambda t, e, c: (t, 0, 0)),
                  pl.BlockSpec((1, d, 2 * D_FF), lambda t, e, c: (e, 0, 0)),
                  pl.BlockSpec((1, 1, 2 * D_FF), lambda t, e, c: (e, 0, 0)),
                  pl.BlockSpec((1, D_FF, d), lambda t, e, c: (e, 0, 0)),
                  pl.BlockSpec((1, 1, d), lambda t, e, c: (e, 0, 0)),
                  pl.BlockSpec((1, 1, d), lambda t, e, c: ((t * grp) // tiles_per_b, 0, 0))],
        out_specs=pl.BlockSpec((grp * tt, d), lambda t, e, c: (t, 0)),
    )
    return pl.pallas_call(
        kern,
        grid_spec=grid_spec,
        out_shape=jax.ShapeDtypeStruct((n, d), F32),
        compiler_params=_cparams("parallel", "arbitrary"),
        name="sparse_moe",
    )(counts, h2, x1f, rcol, rt, gt, prep["w_mlp1"], prep["b_mlp1"], prep["w_mlp2"], prep["b_mlp2"], g2)


DEINT_BLOCK = 2 * LANES


def _deint_kernel(w_ref, perm_ref, o_ref):
    perm = perm_ref[...]
    half = DEINT_BLOCK // 2
    for blk in range(2 * D_FF // DEINT_BLOCK):
        wb = w_ref[0, :, blk * DEINT_BLOCK:(blk + 1) * DEINT_BLOCK].astype(BF16)
        res = jnp.dot(wb, perm, preferred_element_type=F32).astype(BF16)
        o_ref[0, :, blk * half:(blk + 1) * half] = res[:, :half]
        o_ref[0, :, D_FF + blk * half:D_FF + (blk + 1) * half] = res[:, half:]


def _deinterleave_mlp1(w_mlp1):
    idx = jnp.arange(DEINT_BLOCK)
    src = jnp.where(idx < DEINT_BLOCK // 2, 2 * idx, 2 * (idx - DEINT_BLOCK // 2) + 1)
    perm = (idx[:, None] == src[None, :]).astype(BF16)
    blk = (1, D_MODEL, 2 * D_FF)
    return pl.pallas_call(
        _deint_kernel,
        grid=(N_EXPERTS,),
        in_specs=[pl.BlockSpec(blk, lambda e: (e, 0, 0)), _full((DEINT_BLOCK, DEINT_BLOCK))],
        out_specs=pl.BlockSpec(blk, lambda e: (e, 0, 0)),
        out_shape=jax.ShapeDtypeStruct(w_mlp1.shape, BF16),
        compiler_params=_cparams("parallel"),
        name="mlp1_regroup",
    )(w_mlp1, perm)


def _ssm_discretise(a_re, a_im, log_step, b_re, b_im):
    a_re = jnp.minimum(a_re, -SSM_MIN_DECAY)
    dt = jnp.exp(log_step)[:, None]
    mag = jnp.exp(a_re * dt)
    ab_re = mag * jnp.cos(a_im * dt)
    ab_im = mag * jnp.sin(a_im * dt)
    den = a_re * a_re + a_im * a_im
    xr = ab_re - 1.0
    z_re = (xr * a_re + ab_im * a_im) / den
    z_im = (ab_im * a_re - xr * a_im) / den
    bb_re = z_re[..., None] * b_re - z_im[..., None] * b_im
    bb_im = z_re[..., None] * b_im + z_im[..., None] * b_re
    return ab_re, ab_im, bb_re, bb_im


def _prepare(max_s, norm1_g, norm2_g, w_in, q_norm_g, k_norm_g, ssm_a_re, ssm_a_im, ssm_log_step,
             ssm_b_re, ssm_b_im, ssm_c_re, ssm_c_im, ssm_d, w_glu, b_glu, w_out, w_router, b_router,
             w_mlp1, b_mlp1, w_mlp2, b_mlp2):
    g, p, c = SSM_GROUPS, SSM_STATE, SSM_GROUP_CH
    eye = jnp.eye(g, dtype=F32)
    a_l, b_l, c_l = [], [], []
    for dr in range(2):
        ab_re, ab_im, bb_re, bb_im = _ssm_discretise(ssm_a_re[dr], ssm_a_im[dr], ssm_log_step[dr],
                                                     ssm_b_re[dr], ssm_b_im[dr])
        a_l.append(jnp.stack([ab_re.reshape(g * p), ab_im.reshape(g * p)]))
        bd = lambda blk: jnp.einsum("gpc,gh->gchp", blk, eye).reshape(g * c, g * p)
        b_l.append(jnp.concatenate([bd(bb_re), bd(bb_im)], axis=1).astype(BF16))
        cd = lambda blk: jnp.einsum("gcp,gh->gphc", blk, eye).reshape(g * p, g * c)
        c_l.append(jnp.concatenate([cd(ssm_c_re[dr]), -cd(ssm_c_im[dr])], axis=0).astype(BF16))

    pos = jnp.arange(max_s, dtype=F32)
    inv = 1.0 / (ROPE_THETA ** (jnp.arange(0, ROT_DIM, 2, dtype=F32) / ROT_DIM))
    ang = pos[:, None] * inv[None, :]
    cos, sin = jnp.cos(ang), jnp.sin(ang)
    half = ROT_DIM // 2
    rest = HEAD_DIM - ROT_DIM
    zeros = lambda w: jnp.zeros((max_s, w), F32)
    per_head = lambda t: jnp.tile(t, (1, LANES // HEAD_DIM))
    rope_c = per_head(jnp.concatenate([cos, cos, jnp.ones((max_s, rest), F32)], axis=1))
    rope_s1 = per_head(jnp.concatenate([-sin, zeros(half + rest)], axis=1))
    rope_s2 = per_head(jnp.concatenate([zeros(half), sin, zeros(rest)], axis=1))

    idx = jnp.arange(2 * LANES)
    head_ind = (idx[:, None] // HEAD_DIM == idx[None, :] // HEAD_DIM).astype(BF16)
    head_expand = (jnp.arange(LANES)[:, None] == jnp.arange(ATT_WIDTH)[None, :] // HEAD_DIM).astype(BF16)
    ridx = jnp.arange(RANK_BLOCK)
    ltri = (ridx[None, :] < ridx[:, None]).astype(BF16)

    return dict(
        norm1_g=norm1_g.reshape(1, D_MODEL), norm2_g=norm2_g.reshape(1, D_MODEL),
        w_in=w_in.astype(BF16),
        q_gain=jnp.tile(q_norm_g, (1, ATT_HEADS)) * (HEAD_DIM ** -0.5),
        k_gain=jnp.tile(k_norm_g, (1, ATT_HEADS)),
        rope_c=rope_c, rope_s1=rope_s1, rope_s2=rope_s2, head_ind=head_ind, head_expand=head_expand,
        ltri=ltri, ssm_a=a_l, ssm_b=b_l, ssm_c=c_l,
        ssm_d=ssm_d.reshape(1, SSM_WIDTH), w_glu=w_glu.astype(BF16), b_glu=b_glu.reshape(1, SSM_WIDTH),
        w_out_att=w_out[:ATT_WIDTH].astype(BF16), w_out_ssm=w_out[ATT_WIDTH:].astype(BF16),
        w_router=jnp.pad(w_router, ((0, 0), (0, LANES - N_EXPERTS))),
        b_router=jnp.pad(b_router, (0, LANES - N_EXPERTS)).reshape(1, LANES),
        w_mlp1=_deinterleave_mlp1(w_mlp1),
        b_mlp1=jnp.concatenate([b_mlp1[..., 0::2], b_mlp1[..., 1::2]], axis=-1).reshape(N_EXPERTS, 1, 2 * D_FF),
        w_mlp2=w_mlp2.astype(BF16), b_mlp2=b_mlp2.reshape(N_EXPERTS, 1, D_MODEL),
    )


def _moe_tiles(s):
    tt = min(1024, s)
    first = max(16, tt // 8)
    spill = max(16, tt // 32)
    sizes = [first, spill]
    while sum(sizes) < tt:
        sizes.append(min(first, tt - sum(sizes)))
    grp = 2 if (s // tt) % 2 == 0 else 1
    return tt, tuple(sizes), grp


def _encoder_layer(x, mod, prep):
    b, s, d = x.shape
    sh1, sc1, g1, sh2, sc2, g2 = [m.reshape(b, 1, d) for m in jnp.split(mod, N_MOD, axis=-1)]
    qkv0, qkv1, qkv2, u_t = _in_projection(x, sc1, sh1, prep)
    outs, lses = [], []
    for qkv, (_, dil) in zip((qkv0, qkv1, qkv2), DILATION_GROUPS):
        o, lse = _band_attention(qkv, dil)
        outs.append(o)
        lses.append(lse)
    yf, yb = _ssm_scan(u_t, prep, b)
    x1 = _post_mix(x, outs, lses, yf, yb, u_t, g1, prep)
    tt, mc, grp = _moe_tiles(s)
    x1f = x1.reshape(b * s, d)
    h2, rcol, rt, gt, cnt = _router(x1f, sc2, sh2, prep, tt, s // tt)
    counts = cnt[:, 0, :N_EXPERTS].reshape(-1)
    out = _moe(counts, h2, x1f, rcol, rt, gt, g2, prep, tt, mc, grp, s // tt)
    return out.reshape(b, s, d)


def kernel(x_prompt, x_sample, c_prompt, c_sample, w_ada, b_ada, norm1_g, norm2_g, w_in, q_norm_g, k_norm_g, ssm_a_re, ssm_a_im, ssm_log_step, ssm_b_re, ssm_b_im, ssm_c_re, ssm_c_im, ssm_d, w_glu, b_glu, w_out, w_router, b_router, w_mlp1, b_mlp1, w_mlp2, b_mlp2):
    depth = w_ada.shape[0]
    y_prompt, y_sample = x_prompt, x_sample
    nbp = x_prompt.shape[0]
    max_s = max(x_prompt.shape[1], x_sample.shape[1])
    for l in range(depth):
        prep = _prepare(max_s, norm1_g[l], norm2_g[l], w_in[l], q_norm_g[l], k_norm_g[l], ssm_a_re[l],
                        ssm_a_im[l], ssm_log_step[l], ssm_b_re[l], ssm_b_im[l], ssm_c_re[l], ssm_c_im[l],
                        ssm_d[l], w_glu[l], b_glu[l], w_out[l], w_router[l], b_router[l], w_mlp1[l],
                        b_mlp1[l], w_mlp2[l], b_mlp2[l])
        mod = _modulation(jnp.concatenate([c_prompt, c_sample], axis=0), w_ada[l], b_ada[l])
        y_prompt = _encoder_layer(y_prompt, mod[:nbp], prep)
        y_sample = _encoder_layer(y_sample, mod[nbp:], prep)
    return (y_prompt, y_sample)
```

```python
import functools
import math

import jax
import jax.numpy as jnp
from jax import lax
from jax.experimental import pallas as pl
from jax.experimental.pallas import tpu as pltpu

F32 = jnp.float32
BF16 = jnp.bfloat16
HIGHEST = lax.Precision.HIGHEST

D_MODEL = 1024
HEAD_DIM = 64
ATT_HEADS = 8
DILATION_GROUPS = ((128, 1), (512, 4), (2048, 16))
N_DIL = len(DILATION_GROUPS)
ATT_WIDTH = ATT_HEADS * HEAD_DIM
QKV_WIDTH = 3 * ATT_WIDTH
BAND_RADIUS = 64
ROT_DIM = HEAD_DIM // 4
ROPE_THETA = 500000.0
SSM_GROUP_CH = 16
SSM_GROUPS = 16
SSM_WIDTH = SSM_GROUPS * SSM_GROUP_CH
SSM_STATE = 64
SSM_LANES = SSM_GROUPS * SSM_STATE
SSM_MIN_DECAY = 1e-4
ATT_COLS = N_DIL * QKV_WIDTH
N_EXPERTS = 32
TOP_K = 4
D_FF = D_MODEL
SWIGLU_LIMIT = 7.0
SWIGLU_ALPHA = 1.702
N_MOD = 6
EPS = 1e-6
MASK_VALUE = -1e30

LANES = 128
VMEM_LIMIT = 56 * 1024 * 1024


def _cparams(*sem):
    return pltpu.CompilerParams(dimension_semantics=sem, vmem_limit_bytes=VMEM_LIMIT)


def _full(shape):
    n = len(shape)
    return pl.BlockSpec(shape, lambda *_: (0,) * n)


def _full_once(shape):
    n = len(shape)
    return pl.BlockSpec(shape, lambda *_: (0,) * n, pipeline_mode=pl.Buffered(1))


def _mod_kernel(c_ref, w_ref, b_ref, o_ref):
    c = c_ref[...]
    a = c * jax.nn.sigmoid(c)
    o_ref[...] = jnp.dot(a, w_ref[...], precision=HIGHEST, preferred_element_type=F32) + b_ref[...]


def _modulation(c_all, w_ada, b_ada):
    nb = c_all.shape[0]
    ncol = w_ada.shape[1]
    bw = D_MODEL
    return pl.pallas_call(
        _mod_kernel,
        grid=(ncol // bw,),
        in_specs=[_full((nb, D_MODEL)),
                  pl.BlockSpec((D_MODEL, bw), lambda j: (0, j)),
                  pl.BlockSpec((1, bw), lambda j: (0, j))],
        out_specs=pl.BlockSpec((nb, bw), lambda j: (0, j)),
        out_shape=jax.ShapeDtypeStruct((nb, ncol), F32),
        compiler_params=_cparams("arbitrary"),
        name="modulation",
    )(c_all, w_ada, b_ada.reshape(1, ncol))


def _head_rms(t, gmat):
    sq = (t * t).astype(BF16)
    half = 2 * LANES
    ss = jnp.concatenate(
        [jnp.dot(sq[:, :half], gmat, preferred_element_type=F32),
         jnp.dot(sq[:, half:], gmat, preferred_element_type=F32)], axis=1)
    return t * lax.rsqrt(ss * (1.0 / HEAD_DIM) + EPS)


def _rope(t, cs, s1, s2):
    return (t * cs + pltpu.roll(t, ATT_WIDTH - ROT_DIM // 2, 1) * s1
            + pltpu.roll(t, ROT_DIM // 2, 1) * s2)


def _store_dilated(o_ref, tmp_ref, col, val, dil):
    if dil == 1:
        o_ref[0, :, col:col + ATT_WIDTH] = val.astype(BF16)
        return
    rows = val.shape[0] // dil
    nchunk = ATT_WIDTH // LANES
    for c in range(nchunk):
        tmp_ref[c] = val[:, c * LANES:(c + 1) * LANES]
    for r in range(dil):
        c0 = r * QKV_WIDTH + col
        picked = [tmp_ref[c, pl.ds(r, rows, stride=dil), :] for c in range(nchunk)]
        o_ref[0, :, c0:c0 + ATT_WIDTH] = jnp.concatenate(picked, axis=1).astype(BF16)


def _proj_kernel(x_ref, sc_ref, sh_ref, g_ref, w_ref, qg_ref, kg_ref, cs_ref, s1_ref, s2_ref, gmat_ref,
                 o0_ref, o1_ref, o2_ref, u_ref, tq_ref, tk_ref, tv_ref):
    x = x_ref[0]
    ms = jnp.mean(x * x, axis=-1, keepdims=True)
    h = (x * lax.rsqrt(ms + EPS) * g_ref[...]) * (1.0 + sc_ref[0]) + sh_ref[0]
    hb = h.astype(BF16)
    rep = ATT_WIDTH // LANES
    cs = jnp.concatenate([cs_ref[...]] * rep, axis=1)
    s1 = jnp.concatenate([s1_ref[...]] * rep, axis=1)
    s2 = jnp.concatenate([s2_ref[...]] * rep, axis=1)
    gmat = gmat_ref[...]
    for g, o_ref in enumerate((o0_ref, o1_ref, o2_ref)):
        p = jnp.dot(hb, w_ref[:, g * QKV_WIDTH:(g + 1) * QKV_WIDTH], preferred_element_type=F32)
        q = _rope(_head_rms(p[:, :ATT_WIDTH], gmat) * qg_ref[g:g + 1, :], cs, s1, s2)
        k = _rope(_head_rms(p[:, ATT_WIDTH:2 * ATT_WIDTH], gmat) * kg_ref[g:g + 1, :], cs, s1, s2)
        dil = DILATION_GROUPS[g][1]
        _store_dilated(o_ref, tq_ref, 0, q, dil)
        _store_dilated(o_ref, tk_ref, ATT_WIDTH, k, dil)
        _store_dilated(o_ref, tv_ref, 2 * ATT_WIDTH, p[:, 2 * ATT_WIDTH:], dil)
    u_ref[...] = jnp.dot(hb, w_ref[:, ATT_COLS:], preferred_element_type=F32)


def _in_projection(x, sc1, sh1, prep):
    b, s, d = x.shape
    ts = min(512, s)
    qkv_shape = [jax.ShapeDtypeStruct((b, s // dl, dl * QKV_WIDTH), BF16) for _, dl in DILATION_GROUPS]
    qkv_spec = [pl.BlockSpec((1, ts // dl, dl * QKV_WIDTH), lambda bi, si: (bi, si, 0))
                for _, dl in DILATION_GROUPS]
    vec_spec = pl.BlockSpec((1, 1, d), lambda bi, si: (bi, 0, 0))
    tab_spec = pl.BlockSpec((ts, LANES), lambda bi, si: (si, 0))
    return pl.pallas_call(
        _proj_kernel,
        grid=(b, s // ts),
        in_specs=[pl.BlockSpec((1, ts, d), lambda bi, si: (bi, si, 0)), vec_spec, vec_spec,
                  _full((1, d)), _full_once(prep["w_in"].shape),
                  _full((N_DIL, ATT_WIDTH)), _full((N_DIL, ATT_WIDTH)),
                  tab_spec, tab_spec, tab_spec, _full((2 * LANES, 2 * LANES))],
        out_specs=qkv_spec + [pl.BlockSpec((ts, SSM_WIDTH), lambda bi, si: (si, bi))],
        out_shape=qkv_shape + [jax.ShapeDtypeStruct((s, b * SSM_WIDTH), F32)],
        scratch_shapes=[pltpu.VMEM((ATT_WIDTH // LANES, ts, LANES), F32)] * 3,
        compiler_params=_cparams("parallel", "parallel"),
        name="in_projection",
    )(x, sc1, sh1, prep["norm1_g"], prep["w_in"], prep["q_gain"], prep["k_gain"],
      prep["rope_c"][:s], prep["rope_s1"][:s], prep["rope_s2"][:s], prep["head_ind"])


ATTN_UNROLL = 4


def _attn_kernel(qkv_ref, o_ref, lse_ref, *, m, qb, win, rp):
    lane = lax.broadcasted_iota(jnp.int32, (qb, LANES), 1)
    lower = lane < HEAD_DIM
    row_i = lax.broadcasted_iota(jnp.int32, (qb, win), 0)
    col_i = lax.broadcasted_iota(jnp.int32, (qb, win), 1)

    for rr in range(rp):
        base = rr * QKV_WIDTH

        def block(j, carry, base=base, rr=rr):
            q0 = pl.multiple_of(j * qb, qb)
            ws = jnp.clip(q0 - BAND_RADIUS, 0, m - win)
            ws = pl.multiple_of(ws, math.gcd(BAND_RADIUS, win))
            valid = jnp.abs((ws + col_i) - (q0 + row_i)) <= BAND_RADIUS
            lse_blk = jnp.zeros((qb, LANES), F32)
            for pair in range(ATT_WIDTH // LANES):
                off = base + pair * LANES
                q2 = qkv_ref[0, pl.ds(q0, qb), off:off + LANES]
                k2 = qkv_ref[0, pl.ds(ws, win), ATT_WIDTH + off:ATT_WIDTH + off + LANES]
                v2 = qkv_ref[0, pl.ds(ws, win), 2 * ATT_WIDTH + off:2 * ATT_WIDTH + off + LANES]
                outs = []
                for a in range(2):
                    head_lanes = lower if a == 0 else jnp.logical_not(lower)
                    qa = jnp.where(head_lanes, q2, jnp.zeros_like(q2))
                    sc = lax.dot_general(qa, k2, (((1,), (1,)), ((), ())), preferred_element_type=F32)
                    sc = jnp.where(valid, sc, MASK_VALUE)
                    mx = jnp.max(sc, axis=1, keepdims=True)
                    p = jnp.exp(sc - mx)
                    den = jnp.sum(p, axis=1, keepdims=True)
                    pv = jnp.dot(p.astype(BF16), v2, preferred_element_type=F32)
                    outs.append(pv * (1.0 / den))
                    lse_blk = jnp.where(lane == 2 * pair + a, mx + jnp.log(den), lse_blk)
                oc = rr * ATT_WIDTH + pair * LANES
                o_ref[0, pl.ds(q0, qb), oc:oc + LANES] = jnp.where(lower, outs[0], outs[1]).astype(BF16)
            lse_ref[0, pl.ds(q0, qb), rr * LANES:(rr + 1) * LANES] = lse_blk
            return carry

        nblk = m // qb
        lax.fori_loop(0, nblk, block, 0, unroll=min(ATTN_UNROLL, nblk))


def _band_attention(view, dil):
    b, m, _ = view.shape
    qb = min(128, m)
    win = min(qb + 2 * BAND_RADIUS, m)
    rp = max(1, min(dil, ATTN_UNROLL * qb // m))
    kern = functools.partial(_attn_kernel, m=m, qb=qb, win=win, rp=rp)
    return pl.pallas_call(
        kern,
        grid=(b, dil // rp),
        in_specs=[pl.BlockSpec((1, m, rp * QKV_WIDTH), lambda bi, r: (bi, 0, r))],
        out_specs=[pl.BlockSpec((1, m, rp * ATT_WIDTH), lambda bi, r: (bi, 0, r)),
                   pl.BlockSpec((1, m, rp * LANES), lambda bi, r: (bi, 0, r))],
        out_shape=[jax.ShapeDtypeStruct((b, m, dil * ATT_WIDTH), BF16),
                   jax.ShapeDtypeStruct((b, m, dil * LANES), F32)],
        compiler_params=_cparams("parallel", "parallel"),
        name=f"band_attention_d{dil}",
    )(view)


SCAN_LANES = 256


def _ssm_kernel(uf_ref, ub_ref, bf_ref, bb_ref, cf_ref, cb_ref, af_ref, ab_ref, yf_ref, yb_ref,
                hf_s, hb_s, carf_s, carb_s, *, nb, tc):
    rows = nb * tc

    @pl.when(pl.program_id(0) == 0)
    def _():
        carf_s[...] = jnp.zeros_like(carf_s)
        carb_s[...] = jnp.zeros_like(carb_s)

    hf_s[...] = jnp.dot(uf_ref[...].reshape(rows, SSM_WIDTH).astype(BF16), bf_ref[...],
                        preferred_element_type=F32)
    hb_s[...] = jnp.dot(ub_ref[...].reshape(rows, SSM_WIDTH).astype(BF16), bb_ref[...],
                        preferred_element_type=F32)

    for c in range(SSM_LANES // SCAN_LANES):
        re = slice(c * SCAN_LANES, (c + 1) * SCAN_LANES)
        im = slice(SSM_LANES + c * SCAN_LANES, SSM_LANES + (c + 1) * SCAN_LANES)
        far = jnp.broadcast_to(af_ref[0:1, re], (nb, SCAN_LANES))
        fai = jnp.broadcast_to(af_ref[1:2, re], (nb, SCAN_LANES))
        bar = jnp.broadcast_to(ab_ref[0:1, re], (nb, SCAN_LANES))
        bai = jnp.broadcast_to(ab_ref[1:2, re], (nb, SCAN_LANES))

        def step(t, carry):
            fr, fi, br, bi = carry
            rf = pl.ds(pl.multiple_of(t * nb, nb), nb)
            rb = pl.ds(pl.multiple_of((tc - 1 - t) * nb, nb), nb)
            nfr = far * fr - fai * fi + hf_s[rf, re]
            nfi = far * fi + fai * fr + hf_s[rf, im]
            nbr = bar * br - bai * bi + hb_s[rb, re]
            nbi = bar * bi + bai * br + hb_s[rb, im]
            hf_s[rf, re] = nfr
            hf_s[rf, im] = nfi
            hb_s[rb, re] = nbr
            hb_s[rb, im] = nbi
            return nfr, nfi, nbr, nbi

        fr, fi, br, bi = lax.fori_loop(
            0, tc, step, (carf_s[:, re], carf_s[:, im], carb_s[:, re], carb_s[:, im]), unroll=2)
        carf_s[:, re] = fr
        carf_s[:, im] = fi
        carb_s[:, re] = br
        carb_s[:, im] = bi

    yf_ref[...] = jnp.dot(hf_s[...].astype(BF16), cf_ref[...],
                          preferred_element_type=F32).reshape(tc, nb, SSM_WIDTH)
    yb_ref[...] = jnp.dot(hb_s[...].astype(BF16), cb_ref[...],
                          preferred_element_type=F32).reshape(tc, nb, SSM_WIDTH)


def _ssm_scan(u_t, prep, nb):
    s = u_t.shape[0]
    u3 = u_t.reshape(s, nb, SSM_WIDTH)
    tc = min(1024 // nb, s)
    nchunk = s // tc
    kern = functools.partial(_ssm_kernel, nb=nb, tc=tc)
    blk = (tc, nb, SSM_WIDTH)
    yf, yb = pl.pallas_call(
        kern,
        grid=(nchunk,),
        in_specs=[pl.BlockSpec(blk, lambda i: (i, 0, 0)),
                  pl.BlockSpec(blk, lambda i: (nchunk - 1 - i, 0, 0)),
                  _full((SSM_WIDTH, 2 * SSM_LANES)), _full((SSM_WIDTH, 2 * SSM_LANES)),
                  _full((2 * SSM_LANES, SSM_WIDTH)), _full((2 * SSM_LANES, SSM_WIDTH)),
                  _full((2, SSM_LANES)), _full((2, SSM_LANES))],
        out_specs=[pl.BlockSpec(blk, lambda i: (i, 0, 0)),
                   pl.BlockSpec(blk, lambda i: (nchunk - 1 - i, 0, 0))],
        out_shape=[jax.ShapeDtypeStruct((s, nb, SSM_WIDTH), F32)] * 2,
        scratch_shapes=[pltpu.VMEM((nb * tc, 2 * SSM_LANES), F32),
                        pltpu.VMEM((nb * tc, 2 * SSM_LANES), F32),
                        pltpu.VMEM((nb, 2 * SSM_LANES), F32),
                        pltpu.VMEM((nb, 2 * SSM_LANES), F32)],
        compiler_params=_cparams("arbitrary"),
        name="s5_scan",
    )(u3, u3, prep["ssm_b"][0], prep["ssm_b"][1], prep["ssm_c"][0], prep["ssm_c"][1],
      prep["ssm_a"][0], prep["ssm_a"][1])
    return yf.reshape(s, nb * SSM_WIDTH), yb.reshape(s, nb * SSM_WIDTH)


def _split_dot(w, mat):
    hi = w.astype(BF16)
    lo = (w - hi.astype(F32)).astype(BF16)
    return (jnp.dot(hi, mat, preferred_element_type=F32)
            + jnp.dot(lo, mat, preferred_element_type=F32))


def _natural_order(ref, tmp_ref, dil, width):
    if dil == 1:
        return ref[0].astype(F32)
    rows = ref.shape[1]
    nchunk = width // LANES
    for r in range(dil):
        blk = ref[0, :, r * width:(r + 1) * width].astype(F32)
        for c in range(nchunk):
            tmp_ref[c, pl.ds(r, rows, stride=dil), :] = blk[:, c * LANES:(c + 1) * LANES]
    return jnp.concatenate([tmp_ref[c] for c in range(nchunk)], axis=1)


def _post_kernel(x_ref, o0_ref, o1_ref, o2_ref, l0_ref, l1_ref, l2_ref, yf_ref, yb_ref, u_ref,
                 g1_ref, dvec_ref, wglu_ref, bglu_ref, woa_ref, wos_ref, emat_ref, x1_ref,
                 to1_ref, to2_ref, tl1_ref, tl2_ref):
    dils = [dl for _, dl in DILATION_GROUPS]
    lses = [_natural_order(r, t, dl, LANES)
            for r, t, dl in zip((l0_ref, l1_ref, l2_ref), (None, tl1_ref, tl2_ref), dils)]
    outs = [_natural_order(r, t, dl, ATT_WIDTH)
            for r, t, dl in zip((o0_ref, o1_ref, o2_ref), (None, to1_ref, to2_ref), dils)]
    top = jnp.maximum(jnp.maximum(lses[0], lses[1]), lses[2])
    es = [jnp.exp(l - top) for l in lses]
    inv = 1.0 / (es[0] + es[1] + es[2])
    emat = emat_ref[...]
    att = None
    for e, o in zip(es, outs):
        term = _split_dot(e * inv, emat) * o
        att = term if att is None else att + term
    y = yf_ref[...] + yb_ref[...] + dvec_ref[...] * u_ref[...]
    y = 0.5 * y * (1.0 + jnp.tanh(math.sqrt(2.0 / math.pi) * (y + 0.044715 * (y * y * y))))
    z = jnp.dot(y.astype(BF16), wglu_ref[...], preferred_element_type=F32) + bglu_ref[...]
    ssm = y * jax.nn.sigmoid(z)
    mix = (jnp.dot(att.astype(BF16), woa_ref[...], preferred_element_type=F32)
           + jnp.dot(ssm.astype(BF16), wos_ref[...], preferred_element_type=F32))
    x1_ref[0] = x_ref[0] + g1_ref[0] * mix


def _post_mix(x, outs, lses, yf, yb, u_t, g1, prep):
    b, s, d = x.shape
    ts = min(512, s)
    tok = lambda w, dl=1: pl.BlockSpec((1, ts // dl, dl * w), lambda bi, si: (bi, si, 0))
    tmaj = pl.BlockSpec((ts, SSM_WIDTH), lambda bi, si: (si, bi))
    dils = [dl for _, dl in DILATION_GROUPS]
    return pl.pallas_call(
        _post_kernel,
        grid=(b, s // ts),
        in_specs=[tok(d)] + [tok(ATT_WIDTH, dl) for dl in dils] + [tok(LANES, dl) for dl in dils]
                 + [tmaj, tmaj, tmaj,
                  pl.BlockSpec((1, 1, d), lambda bi, si: (bi, 0, 0)),
                  _full((1, SSM_WIDTH)), _full((SSM_WIDTH, SSM_WIDTH)), _full((1, SSM_WIDTH)),
                  _full((ATT_WIDTH, d)), _full((SSM_WIDTH, d)), _full((LANES, ATT_WIDTH))],
        out_specs=tok(d),
        out_shape=jax.ShapeDtypeStruct((b, s, d), F32),
        scratch_shapes=([pltpu.VMEM((ATT_WIDTH // LANES, ts, LANES), F32)] * 2
                        + [pltpu.VMEM((1, ts, LANES), F32)] * 2),
        compiler_params=_cparams("parallel", "parallel"),
        name="post_mix",
    )(x, *outs, *lses, yf, yb, u_t, g1, prep["ssm_d"], prep["w_glu"], prep["b_glu"],
      prep["w_out_att"], prep["w_out_ssm"], prep["head_expand"])


RANK_BLOCK = 256


def _router_kernel(x1_ref, sc_ref, sh_ref, g_ref, wr_ref, br_ref, ltri_ref,
                   h2_ref, rcol_ref, rt_ref, gt_ref, cnt_ref, *, tt):
    x = x1_ref[...]
    ms = jnp.mean(x * x, axis=-1, keepdims=True)
    h = (x * lax.rsqrt(ms + EPS) * g_ref[...]) * (1.0 + sc_ref[0]) + sh_ref[0]
    h2_ref[...] = h.astype(BF16)
    lane = lax.broadcasted_iota(jnp.int32, (tt, LANES), 1)
    logits = jnp.dot(h, wr_ref[...], precision=HIGHEST, preferred_element_type=F32) + br_ref[...]
    work = jnp.where(lane < N_EXPERTS, logits, -jnp.inf)
    hots, vals = [], []
    for _ in range(TOP_K):
        mx = jnp.max(work, axis=1, keepdims=True)
        idx = jnp.min(jnp.where(work == mx, lane, LANES), axis=1, keepdims=True)
        hot = lane == idx
        hots.append(hot)
        vals.append(mx)
        work = jnp.where(hot, -jnp.inf, work)
    ws = [jnp.exp(v - vals[0]) for v in vals]
    inv = 1.0 / (ws[0] + ws[1] + ws[2] + ws[3])
    gates = jnp.zeros((tt, LANES), F32)
    sel = jnp.zeros((tt, LANES), F32)
    for hot, w in zip(hots, ws):
        gates = jnp.where(hot, w * inv, gates)
        sel = jnp.where(hot, 1.0, sel)
    ltri = ltri_ref[...]
    carry = jnp.zeros((1, LANES), F32)
    parts = []
    for blk in range(tt // RANK_BLOCK):
        sb = sel[blk * RANK_BLOCK:(blk + 1) * RANK_BLOCK]
        parts.append(jnp.dot(ltri, sb.astype(BF16), preferred_element_type=F32) + carry)
        carry = carry + jnp.sum(sb, axis=0, keepdims=True)
    rank = jnp.where(sel > 0.0, jnp.concatenate(parts, axis=0), -1.0)
    rcol_ref[...] = rank
    rt_ref[0] = rank.T[:N_EXPERTS]
    gt_ref[0] = gates.T[:N_EXPERTS]
    cnt_ref[0] = carry.astype(jnp.int32)


def _router(x1f, sc2, sh2, prep, tt, tiles_per_b):
    n, d = x1f.shape
    nt = n // tt
    vec = pl.BlockSpec((1, 1, d), lambda t: (t // tiles_per_b, 0, 0))
    kern = functools.partial(_router_kernel, tt=tt)
    return pl.pallas_call(
        kern,
        grid=(nt,),
        in_specs=[pl.BlockSpec((tt, d), lambda t: (t, 0)), vec, vec, _full((1, d)),
                  _full((d, LANES)), _full((1, LANES)), _full((RANK_BLOCK, RANK_BLOCK))],
        out_specs=[pl.BlockSpec((tt, d), lambda t: (t, 0)),
                   pl.BlockSpec((tt, LANES), lambda t: (t, 0)),
                   pl.BlockSpec((1, N_EXPERTS, tt), lambda t: (t, 0, 0)),
                   pl.BlockSpec((1, N_EXPERTS, tt), lambda t: (t, 0, 0)),
                   pl.BlockSpec((1, 1, LANES), lambda t: (t, 0, 0))],
        out_shape=[jax.ShapeDtypeStruct((n, d), BF16),
                   jax.ShapeDtypeStruct((n, LANES), F32),
                   jax.ShapeDtypeStruct((nt, N_EXPERTS, tt), F32),
                   jax.ShapeDtypeStruct((nt, N_EXPERTS, tt), F32),
                   jax.ShapeDtypeStruct((nt, 1, LANES), jnp.int32)],
        compiler_params=_cparams("parallel"),
        name="router",
    )(x1f, sc2, sh2, prep["norm2_g"], prep["w_router"], prep["b_router"], prep["ltri"])


def _moe_kernel(cnt_ref, h2_ref, x1_ref, rcol_ref, rt_ref, gt_ref, w1_ref, b1_ref, w2_ref, b2_ref,
                g2_ref, out_ref, *, tt, mc, grp):
    t = pl.program_id(0)
    e = pl.program_id(1)

    @pl.when(e == 0)
    def _():
        out_ref[...] = x1_ref[...]

    lane = lax.broadcasted_iota(jnp.int32, (tt, LANES), 1)
    mcp = -(-mc // LANES) * LANES
    ones = jnp.ones((LANES, mcp), BF16)

    for sub in range(grp):
        rows = slice(sub * tt, (sub + 1) * tt)
        n = cnt_ref[(t * grp + sub) * N_EXPERTS + e]
        rsel = jnp.where(lane == e, rcol_ref[rows, :], 0.0)
        rhi = rsel.astype(BF16)
        rlo = (rsel - rhi.astype(F32)).astype(BF16)
        rcol = (jnp.dot(rhi, ones, preferred_element_type=F32)
                + jnp.dot(rlo, ones, preferred_element_type=F32))[:, :mc]
        rrow = rt_ref[sub, pl.ds(e, 1), :]
        grow = gt_ref[sub, pl.ds(e, 1), :]

        for c in range(-(-tt // mc)):
            base = float(c * mc)

            @pl.when(c * mc < n)
            def _(base=base, rows=rows, rcol=rcol, rrow=rrow, grow=grow):
                ri = lax.broadcasted_iota(jnp.int32, (mc, tt), 0).astype(F32) + base
                hit = rrow == ri
                xg = jnp.dot(jnp.where(hit, 1.0, 0.0).astype(BF16), h2_ref[rows, :],
                             preferred_element_type=F32).astype(BF16)
                hh = jnp.dot(xg, w1_ref[0], preferred_element_type=F32) + b1_ref[0]
                glu = jnp.minimum(hh[:, :D_FF], SWIGLU_LIMIT)
                lin = jnp.clip(hh[:, D_FF:], -SWIGLU_LIMIT, SWIGLU_LIMIT)
                act = glu * jax.nn.sigmoid(SWIGLU_ALPHA * glu) * (lin + 1.0)
                y = jnp.dot(act.astype(BF16), w2_ref[0], preferred_element_type=F32) + b2_ref[0]
                gate = jnp.sum(jnp.where(hit, grow, 0.0), axis=1, keepdims=True)
                y = (y * gate) * g2_ref[0]
                ci = lax.broadcasted_iota(jnp.int32, (tt, mc), 1).astype(F32) + base
                scat = jnp.where(rcol == ci, 1.0, 0.0).astype(BF16)
                out_ref[rows, :] += jnp.dot(scat, y.astype(BF16), preferred_element_type=F32)


def _moe(counts, h2, x1f, rcol, rt, gt, g2, prep, tt, mc, grp, tiles_per_b):
    n, d = x1f.shape
    nt = n // (tt * grp)
    kern = functools.partial(_moe_kernel, tt=tt, mc=mc, grp=grp)
    once = pl.Buffered(1)
    grid_spec = pltpu.PrefetchScalarGridSpec(
        num_scalar_prefetch=1,
        grid=(nt, N_EXPERTS),
        in_specs=[pl.BlockSpec((grp * tt, d), lambda t, e, c: (t, 0), pipeline_mode=once),
                  pl.BlockSpec((grp * tt, d), lambda t, e, c: (t, 0), pipeline_mode=once),
                  pl.BlockSpec((grp * tt, LANES), lambda t, e, c: (t, 0), pipeline_mode=once),
                  pl.BlockSpec((grp, N_EXPERTS, tt), lambda t, e, c: (t, 0, 0)),
                  pl.BlockSpec((grp, N_EXPERTS, tt), lambda t, e, c: (t, 0, 0)),
                  pl.BlockSpec((1, d, 2 * D_FF), lambda t, e, c: (e, 0, 0)),
                  pl.BlockSpec((1, 1, 2 * D_FF), lambda t, e, c: (e, 0, 0)),
                  pl.BlockSpec((1, D_FF, d), lambda t, e, c: (e, 0, 0)),
                  pl.BlockSpec((1, 1, d), lambda t, e, c: (e, 0, 0)),
                  pl.BlockSpec((1, 1, d), lambda t, e, c: ((t * grp) // tiles_per_b, 0, 0))],
        out_specs=pl.BlockSpec((grp * tt, d), lambda t, e, c: (t, 0)),
    )
    return pl.pallas_call(
        kern,
        grid_spec=grid_spec,
        out_shape=jax.ShapeDtypeStruct((n, d), F32),
        compiler_params=_cparams("parallel", "arbitrary"),
        name="sparse_moe",
    )(counts, h2, x1f, rcol, rt, gt, prep["w_mlp1"], prep["b_mlp1"], prep["w_mlp2"], prep["b_mlp2"], g2)


DEINT_BLOCK = 2 * LANES


def _deint_kernel(w_ref, perm_ref, o_ref):
    perm = perm_ref[...]
    half = DEINT_BLOCK // 2
    for blk in range(2 * D_FF // DEINT_BLOCK):
        wb = w_ref[0, :, blk * DEINT_BLOCK:(blk + 1) * DEINT_BLOCK].astype(BF16)
        res = jnp.dot(wb, perm, preferred_element_type=F32).astype(BF16)
        o_ref[0, :, blk * half:(blk + 1) * half] = res[:, :half]
        o_ref[0, :, D_FF + blk * half:D_FF + (blk + 1) * half] = res[:, half:]


def _deinterleave_mlp1(w_mlp1):
    idx = jnp.arange(DEINT_BLOCK)
    src = jnp.where(idx < DEINT_BLOCK // 2, 2 * idx, 2 * (idx - DEINT_BLOCK // 2) + 1)
    perm = (idx[:, None] == src[None, :]).astype(BF16)
    blk = (1, D_MODEL, 2 * D_FF)
    return pl.pallas_call(
        _deint_kernel,
        grid=(N_EXPERTS,),
        in_specs=[pl.BlockSpec(blk, lambda e: (e, 0, 0)), _full((DEINT_BLOCK, DEINT_BLOCK))],
        out_specs=pl.BlockSpec(blk, lambda e: (e, 0, 0)),
        out_shape=jax.ShapeDtypeStruct(w_mlp1.shape, BF16),
        compiler_params=_cparams("parallel"),
        name="mlp1_regroup",
    )(w_mlp1, perm)


def _ssm_discretise(a_re, a_im, log_step, b_re, b_im):
    a_re = jnp.minimum(a_re, -SSM_MIN_DECAY)
    dt = jnp.exp(log_step)[:, None]
    mag = jnp.exp(a_re * dt)
    ab_re = mag * jnp.cos(a_im * dt)
    ab_im = mag * jnp.sin(a_im * dt)
    den = a_re * a_re + a_im * a_im
    xr = ab_re - 1.0
    z_re = (xr * a_re + ab_im * a_im) / den
    z_im = (ab_im * a_re - xr * a_im) / den
    bb_re = z_re[..., None] * b_re - z_im[..., None] * b_im
    bb_im = z_re[..., None] * b_im + z_im[..., None] * b_re
    return ab_re, ab_im, bb_re, bb_im


def _prepare(max_s, norm1_g, norm2_g, w_in, q_norm_g, k_norm_g, ssm_a_re, ssm_a_im, ssm_log_step,
             ssm_b_re, ssm_b_im, ssm_c_re, ssm_c_im, ssm_d, w_glu, b_glu, w_out, w_router, b_router,
             w_mlp1, b_mlp1, w_mlp2, b_mlp2):
    g, p, c = SSM_GROUPS, SSM_STATE, SSM_GROUP_CH
    eye = jnp.eye(g, dtype=F32)
    a_l, b_l, c_l = [], [], []
    for dr in range(2):
        ab_re, ab_im, bb_re, bb_im = _ssm_discretise(ssm_a_re[dr], ssm_a_im[dr], ssm_log_step[dr],
                                                     ssm_b_re[dr], ssm_b_im[dr])
        a_l.append(jnp.stack([ab_re.reshape(g * p), ab_im.reshape(g * p)]))
        bd = lambda blk: jnp.einsum("gpc,gh->gchp", blk, eye).reshape(g * c, g * p)
        b_l.append(jnp.concatenate([bd(bb_re), bd(bb_im)], axis=1).astype(BF16))
        cd = lambda blk: jnp.einsum("gcp,gh->gphc", blk, eye).reshape(g * p, g * c)
        c_l.append(jnp.concatenate([cd(ssm_c_re[dr]), -cd(ssm_c_im[dr])], axis=0).astype(BF16))

    pos = jnp.arange(max_s, dtype=F32)
    inv = 1.0 / (ROPE_THETA ** (jnp.arange(0, ROT_DIM, 2, dtype=F32) / ROT_DIM))
    ang = pos[:, None] * inv[None, :]
    cos, sin = jnp.cos(ang), jnp.sin(ang)
    half = ROT_DIM // 2
    rest = HEAD_DIM - ROT_DIM
    zeros = lambda w: jnp.zeros((max_s, w), F32)
    per_head = lambda t: jnp.tile(t, (1, LANES // HEAD_DIM))
    rope_c = per_head(jnp.concatenate([cos, cos, jnp.ones((max_s, rest), F32)], axis=1))
    rope_s1 = per_head(jnp.concatenate([-sin, zeros(half + rest)], axis=1))
    rope_s2 = per_head(jnp.concatenate([zeros(half), sin, zeros(rest)], axis=1))

    idx = jnp.arange(2 * LANES)
    head_ind = (idx[:, None] // HEAD_DIM == idx[None, :] // HEAD_DIM).astype(BF16)
    head_expand = (jnp.arange(LANES)[:, None] == jnp.arange(ATT_WIDTH)[None, :] // HEAD_DIM).astype(BF16)
    ridx = jnp.arange(RANK_BLOCK)
    ltri = (ridx[None, :] < ridx[:, None]).astype(BF16)

    return dict(
        norm1_g=norm1_g.reshape(1, D_MODEL), norm2_g=norm2_g.reshape(1, D_MODEL),
        w_in=w_in.astype(BF16),
        q_gain=jnp.tile(q_norm_g, (1, ATT_HEADS)) * (HEAD_DIM ** -0.5),
        k_gain=jnp.tile(k_norm_g, (1, ATT_HEADS)),
        rope_c=rope_c, rope_s1=rope_s1, rope_s2=rope_s2, head_ind=head_ind, head_expand=head_expand,
        ltri=ltri, ssm_a=a_l, ssm_b=b_l, ssm_c=c_l,
        ssm_d=ssm_d.reshape(1, SSM_WIDTH), w_glu=w_glu.astype(BF16), b_glu=b_glu.reshape(1, SSM_WIDTH),
        w_out_att=w_out[:ATT_WIDTH].astype(BF16), w_out_ssm=w_out[ATT_WIDTH:].astype(BF16),
        w_router=jnp.pad(w_router, ((0, 0), (0, LANES - N_EXPERTS))),
        b_router=jnp.pad(b_router, (0, LANES - N_EXPERTS)).reshape(1, LANES),
        w_mlp1=_deinterleave_mlp1(w_mlp1),
        b_mlp1=jnp.concatenate([b_mlp1[..., 0::2], b_mlp1[..., 1::2]], axis=-1).reshape(N_EXPERTS, 1, 2 * D_FF),
        w_mlp2=w_mlp2.astype(BF16), b_mlp2=b_mlp2.reshape(N_EXPERTS, 1, D_MODEL),
    )


def _moe_tiles(s):
    tt = min(1024, s)
    mc = 160 if tt == 1024 else max(16, tt // 4)
    grp = 2 if (s // tt) % 2 == 0 else 1
    return tt, mc, grp


def _encoder_layer(x, mod, prep):
    b, s, d = x.shape
    sh1, sc1, g1, sh2, sc2, g2 = [m.reshape(b, 1, d) for m in jnp.split(mod, N_MOD, axis=-1)]
    qkv0, qkv1, qkv2, u_t = _in_projection(x, sc1, sh1, prep)
    outs, lses = [], []
    for qkv, (_, dil) in zip((qkv0, qkv1, qkv2), DILATION_GROUPS):
        o, lse = _band_attention(qkv, dil)
        outs.append(o)
        lses.append(lse)
    yf, yb = _ssm_scan(u_t, prep, b)
    x1 = _post_mix(x, outs, lses, yf, yb, u_t, g1, prep)
    tt, mc, grp = _moe_tiles(s)
    x1f = x1.reshape(b * s, d)
    h2, rcol, rt, gt, cnt = _router(x1f, sc2, sh2, prep, tt, s // tt)
    counts = cnt[:, 0, :N_EXPERTS].reshape(-1)
    out = _moe(counts, h2, x1f, rcol, rt, gt, g2, prep, tt, mc, grp, s // tt)
    return out.reshape(b, s, d)


def kernel(x_prompt, x_sample, c_prompt, c_sample, w_ada, b_ada, norm1_g, norm2_g, w_in, q_norm_g, k_norm_g, ssm_a_re, ssm_a_im, ssm_log_step, ssm_b_re, ssm_b_im, ssm_c_re, ssm_c_im, ssm_d, w_glu, b_glu, w_out, w_router, b_router, w_mlp1, b_mlp1, w_mlp2, b_mlp2):
    depth = w_ada.shape[0]
    y_prompt, y_sample = x_prompt, x_sample
    nbp = x_prompt.shape[0]
    max_s = max(x_prompt.shape[1], x_sample.shape[1])
    for l in range(depth):
        prep = _prepare(max_s, norm1_g[l], norm2_g[l], w_in[l], q_norm_g[l], k_norm_g[l], ssm_a_re[l],
                        ssm_a_im[l], ssm_log_step[l], ssm_b_re[l], ssm_b_im[l], ssm_c_re[l], ssm_c_im[l],
                        ssm_d[l], w_glu[l], b_glu[l], w_out[l], w_router[l], b_router[l], w_mlp1[l],
                        b_mlp1[l], w_mlp2[l], b_mlp2[l])
        mod = _modulation(jnp.concatenate([c_prompt, c_sample], axis=0), w_ada[l], b_ada[l])
        y_prompt = _encoder_layer(y_prompt, mod[:nbp], prep)
        y_sample = _encoder_layer(y_sample, mod[nbp:], prep)
    return (y_prompt, y_sample)
```
